```python
import math
import jax, jax.numpy as jnp
from jax import lax
import numpy as np

D_MODEL = 1024
BATCH = 16
SEQ = 256
DEPTH = 1
DEC_BATCH = 8
DEC_SEQ = 4096
PAST_LEN = 256

GRID_W = 64
D_MIX = D_MODEL
D_ATT = D_MIX // 2
D_CONV = D_MIX - D_ATT
HEAD_DIM = 64
V_DIM = 2 * HEAD_DIM
N_ATT_HEADS = D_ATT // V_DIM
CONV_W = 3
ROPE_FREQS = HEAD_DIM // 4
ROPE_BASE = 10000.0
Q_BLOCK = 128
EPS = 1e-6
SPLITS = (D_ATT, D_ATT, D_ATT, D_ATT, D_CONV, D_CONV, D_CONV, D_CONV)
D_IN = sum(SPLITS)

kernel_name = "hymba_diffattn_shortconv_dit_step"


def _rms(x, w):
    xf = x.astype(jnp.float32)
    y = xf * lax.rsqrt(jnp.mean(xf * xf, axis=-1, keepdims=True) + EPS)
    return (y * w.astype(jnp.float32)).astype(x.dtype)


def _modulation(cvec, w_ada, b_ada):
    m = jax.nn.silu(cvec) @ w_ada + b_ada
    return jnp.split(m, 3, axis=-1)


def _lambda_init(layer):
    return 0.8 - 0.6 * math.exp(-0.3 * layer)


def _axial_rope_tables(n):
    rows = n // GRID_W
    t = jnp.arange(rows * GRID_W)
    row = (t // GRID_W).astype(jnp.float32)
    col = (t % GRID_W).astype(jnp.float32)
    inv = 1.0 / (ROPE_BASE ** (jnp.arange(ROPE_FREQS, dtype=jnp.float32) / ROPE_FREQS))
    ang_r = row[:, None] * inv
    ang_c = col[:, None] * inv
    shp = (1, n, 1, 1, ROPE_FREQS)
    return (jnp.cos(ang_r).reshape(shp), jnp.sin(ang_r).reshape(shp),
            jnp.cos(ang_c).reshape(shp), jnp.sin(ang_c).reshape(shp))


def _rope_half(x, cos, sin):
    x1, x2 = jnp.split(x, 2, axis=-1)
    cos = cos.astype(x.dtype)
    sin = sin.astype(x.dtype)
    return jnp.concatenate([x1 * cos - x2 * sin, x2 * cos + x1 * sin], axis=-1)


def _apply_axial_rope(x, tables):
    cos_r, sin_r, cos_c, sin_c = tables
    xr, xc = jnp.split(x, 2, axis=-1)
    return jnp.concatenate([_rope_half(xr, cos_r, sin_r), _rope_half(xc, cos_c, sin_c)], axis=-1)


def _diff_attention(q, k, v, lam):
    b, nq = q.shape[0], q.shape[1]
    nblk = nq // Q_BLOCK
    qb = q.reshape(b, nblk, Q_BLOCK, N_ATT_HEADS, 2, HEAD_DIM).swapaxes(0, 1)
    scale = HEAD_DIM ** -0.5

    def one(qblk):
        s = jnp.einsum('bqhmd,bkhmd->bhmqk', qblk, k).astype(jnp.float32) * scale
        p = jax.nn.softmax(s, axis=-1)
        a = p[:, :, 0] - lam * p[:, :, 1]
        return jnp.einsum('bhqk,bkhe->bqhe', a.astype(v.dtype), v)

    out = lax.map(one, qb)
    return out.swapaxes(0, 1).reshape(b, nq, N_ATT_HEADS, V_DIM)


def _short_conv(u, w, b):
    up = jnp.pad(u, ((0, 0), (1, 1), (0, 0)))
    return up[:, :-2] * w[0] + up[:, 1:-1] * w[1] + up[:, 2:] * w[2] + b


def _project(x, shift, scale, norm_w, w_in, q_norm_w, k_norm_w):
    b, n = x.shape[0], x.shape[1]
    h = _rms(x, norm_w) * (1 + scale) + shift
    z = h @ w_in
    idx = list(np.cumsum(SPLITS)[:-1])
    q, k, v, g_att, cb, cc, ch, g_conv = jnp.split(z, idx, axis=-1)
    q = _rms(q.reshape(b, n, N_ATT_HEADS, 2, HEAD_DIM), q_norm_w)
    k = _rms(k.reshape(b, n, N_ATT_HEADS, 2, HEAD_DIM), k_norm_w)
    v = v.reshape(b, n, N_ATT_HEADS, V_DIM)
    return q, k, v, g_att, cb, cc, ch, g_conv


def _finish(x, att, g_att, cb, cc, ch, g_conv, gate, subln_w, conv_w, conv_b, w_out, lam_init):
    b, n = x.shape[0], x.shape[1]
    att = _rms(att, subln_w) * (1.0 - lam_init)
    att = att.reshape(b, n, D_ATT) * jax.nn.silu(g_att)
    conv = cb * _short_conv(cc * ch, conv_w, conv_b) * jax.nn.silu(g_conv)
    y = jnp.concatenate([att, conv], axis=-1) @ w_out
    return x + gate * y


def setup_inputs(seed: int = 0) -> dict:
    key = jax.random.key(seed)
    ks = jax.random.split(key, 20)
    f = jnp.float32
    nrm = lambda k, s, sc: jax.random.normal(k, s, f) * sc
    return {
        "x_prompt": nrm(ks[0], (BATCH, SEQ, D_MODEL), 1.0),
        "x_sample": nrm(ks[1], (DEC_BATCH, DEC_SEQ, D_MODEL), 1.0),
        "cache_k": nrm(ks[2], (DEC_BATCH, DEPTH, PAST_LEN, N_ATT_HEADS, 2, HEAD_DIM), 1.0),
        "cache_v": nrm(ks[3], (DEC_BATCH, DEPTH, PAST_LEN, N_ATT_HEADS, V_DIM), 1.0),
        "c": nrm(ks[4], (DEC_BATCH, D_MODEL), 1.0),
        "c_ctx": nrm(ks[5], (D_MODEL,), 1.0),
        "norm_w": 1.0 + nrm(ks[6], (DEPTH, D_MODEL), 0.02),
        "w_ada": nrm(ks[7], (DEPTH, D_MODEL, 3 * D_MODEL), 0.5 * D_MODEL ** -0.5),
        "b_ada": nrm(ks[8], (DEPTH, 3 * D_MODEL), 0.02),
        "w_in": nrm(ks[9], (DEPTH, D_MODEL, D_IN), D_MODEL ** -0.5),
        "q_norm_w": 1.0 + nrm(ks[10], (DEPTH, HEAD_DIM), 0.02),
        "k_norm_w": 1.0 + nrm(ks[11], (DEPTH, HEAD_DIM), 0.02),
        "lambda_q1": nrm(ks[12], (DEPTH, HEAD_DIM), 0.1),
        "lambda_k1": nrm(ks[13], (DEPTH, HEAD_DIM), 0.1),
        "lambda_q2": nrm(ks[14], (DEPTH, HEAD_DIM), 0.1),
        "lambda_k2": nrm(ks[15], (DEPTH, HEAD_DIM), 0.1),
        "subln_w": 1.0 + nrm(ks[16], (DEPTH, V_DIM), 0.02),
        "conv_w": nrm(ks[17], (DEPTH, CONV_W, D_CONV), CONV_W ** -0.5),
        "conv_b": nrm(ks[18], (DEPTH, D_CONV), 0.02),
        "w_out": nrm(ks[19], (DEPTH, D_MIX, D_MODEL), D_MIX ** -0.5),
    }


def reference(x_prompt, x_sample, cache_k, cache_v, c, c_ctx, norm_w, w_ada, b_ada, w_in,
              q_norm_w, k_norm_w, lambda_q1, lambda_k1, lambda_q2, lambda_k2, subln_w,
              conv_w, conv_b, w_out):
    rope = _axial_rope_tables(x_sample.shape[1])
    xp = x_prompt
    xs = x_sample
    new_k = []
    new_v = []
    for i in range(DEPTH):
        lam_init = _lambda_init(i)
        lam = (jnp.exp(jnp.sum(lambda_q1[i].astype(jnp.float32) * lambda_k1[i].astype(jnp.float32)))
               - jnp.exp(jnp.sum(lambda_q2[i].astype(jnp.float32) * lambda_k2[i].astype(jnp.float32)))
               + lam_init)
        sh_c, sc_c, ga_c = _modulation(c_ctx, w_ada[i], b_ada[i])
        q, k, v, g_att, cb, cc, ch, g_conv = _project(xp, sh_c, sc_c, norm_w[i], w_in[i],
                                                      q_norm_w[i], k_norm_w[i])
        new_k.append(k)
        new_v.append(v)
        att = _diff_attention(q, k, v, lam)
        xp = _finish(xp, att, g_att, cb, cc, ch, g_conv, ga_c, subln_w[i], conv_w[i], conv_b[i],
                     w_out[i], lam_init)
        sh, sc, ga = _modulation(c, w_ada[i], b_ada[i])
        q, k, v, g_att, cb, cc, ch, g_conv = _project(xs, sh[:, None], sc[:, None], norm_w[i],
                                                      w_in[i], q_norm_w[i], k_norm_w[i])
        q = _apply_axial_rope(q, rope)
        k = _apply_axial_rope(k, rope)
        k_all = jnp.concatenate([k, cache_k[:, i].astype(k.dtype)], axis=1)
        v_all = jnp.concatenate([v, cache_v[:, i].astype(v.dtype)], axis=1)
        att = _diff_attention(q, k_all, v_all, lam)
        xs = _finish(xs, att, g_att, cb, cc, ch, g_conv, ga[:, None], subln_w[i], conv_w[i],
                     conv_b[i], w_out[i], lam_init)
    new_cache_k = jnp.stack(new_k, axis=1)
    new_cache_v = jnp.stack(new_v, axis=1)
    return (xp, xs, new_cache_k, new_cache_v)
```

```python
import functools
import math

import jax
import jax.numpy as jnp
from jax import lax
from jax.experimental import pallas as pl
from jax.experimental.pallas import tpu as pltpu

F32 = jnp.float32
BF16 = jnp.bfloat16

HEAD_DIM = 64
V_DIM = 2 * HEAD_DIM
GRID_W = 64
ROPE_FREQS = HEAD_DIM // 4
ROPE_BASE = 10000.0
EPS = 1e-6
CONV_W = 3
LANE = 128
BF16_SUBLANES = 16
VMEM_LIMIT = 52 * 1024 * 1024


def _silu(x):
    return x * (1.0 / (1.0 + jnp.exp(-x)))


def _mod_kernel(c_ref, w_ref, b_ref, lq1, lk1, lq2, lk2, mod_ref, lam_ref, *, lam_init):
    a = _silu(c_ref[...]).astype(BF16)
    m = jnp.dot(a, w_ref[...].astype(BF16), preferred_element_type=F32)
    mod_ref[...] = m + b_ref[...]
    e1 = jnp.exp(jnp.sum(lq1[...] * lk1[...], axis=-1, keepdims=True))
    e2 = jnp.exp(jnp.sum(lq2[...] * lk2[...], axis=-1, keepdims=True))
    lam_ref[...] = jnp.broadcast_to(e1 - e2 + lam_init, lam_ref.shape)


def _modulation(cvecs, w_ada, b_ada, lq1, lk1, lq2, lk2, lam_init):
    rows, d = cvecs.shape
    n_out = w_ada.shape[1]
    bn = 512
    vec = lambda: pl.BlockSpec((1, HEAD_DIM), lambda j: (0, 0))
    return pl.pallas_call(
        functools.partial(_mod_kernel, lam_init=lam_init),
        grid=(n_out // bn,),
        in_specs=[
            pl.BlockSpec((rows, d), lambda j: (0, 0)),
            pl.BlockSpec((d, bn), lambda j: (0, j)),
            pl.BlockSpec((1, bn), lambda j: (0, j)),
            vec(), vec(), vec(), vec(),
        ],
        out_specs=[
            pl.BlockSpec((rows, bn), lambda j: (0, j)),
            pl.BlockSpec((8, LANE), lambda j: (0, 0)),
        ],
        out_shape=[
            jax.ShapeDtypeStruct((rows, n_out), F32),
            jax.ShapeDtypeStruct((8, LANE), F32),
        ],
        name="modulation",
    )(cvecs, w_ada, b_ada, lq1, lk1, lq2, lk2)


def _norm_rope_t(z_t, w_b, rope):
    outs = []
    f = ROPE_FREQS
    for g in range(z_t.shape[0] // HEAD_DIM):
        zg = z_t[g * HEAD_DIM:(g + 1) * HEAD_DIM, :]
        ms = jnp.mean(zg * zg, axis=0, keepdims=True)
        yg = zg * lax.rsqrt(ms + EPS) * w_b
        if rope is not None:
            cr, sr, cc, sc = rope
            x1r, x2r, x1c, x2c = yg[0:f], yg[f:2 * f], yg[2 * f:3 * f], yg[3 * f:4 * f]
            yg = jnp.concatenate([x1r * cr - x2r * sr, x2r * cr + x1r * sr,
                                  x1c * cc - x2c * sc, x2c * cc + x1c * sc], axis=0)
        outs.append(yg)
    return outs


def _proj_kernel(*refs, use_rope, emit_cache):
    it = iter(refs)
    x_ref, shift_ref, scale_ref, nw_ref, w_ref, qw_ref, kw_ref = (next(it) for _ in range(7))
    rope = None
    if use_rope:
        rope = tuple(next(it)[...] for _ in range(4))
    qt_ref, k_ref, vt_ref, ga_ref, u_ref, gc_ref = (next(it) for _ in range(6))
    if emit_cache:
        kf_ref, vf_ref = next(it), next(it)

    x = x_ref[0]
    ms = jnp.mean(x * x, axis=-1, keepdims=True)
    y = x * lax.rsqrt(ms + EPS) * nw_ref[...]
    h = (y * (1.0 + scale_ref[0]) + shift_ref[0]).astype(BF16)

    d_att = qt_ref.shape[1] // 2

    def zcol(i):
        return jnp.dot(h, w_ref[:, i * d_att:(i + 1) * d_att], preferred_element_type=F32)

    q_groups = _norm_rope_t(zcol(0).T, qw_ref[...], rope)
    zero = jnp.zeros_like(q_groups[0])
    for g, yg in enumerate(q_groups):
        yg = yg * (HEAD_DIM ** -0.5)
        pair = [yg, zero] if g % 2 == 0 else [zero, yg]
        qt_ref[0, g * V_DIM:(g + 1) * V_DIM, :] = jnp.concatenate(pair, axis=0).astype(BF16)

    k = jnp.concatenate(_norm_rope_t(zcol(1).T, kw_ref[...], rope), axis=0).T
    k_ref[0] = k.astype(BF16)
    v = zcol(2)
    vt_ref[0] = v.T.astype(BF16)
    if emit_cache:
        kf_ref[0] = k
        vf_ref[0] = v

    ga_ref[0] = _silu(zcol(3)).astype(BF16)
    cb = zcol(4)
    u_ref[0] = (zcol(5) * zcol(6)).astype(BF16)
    gc_ref[0] = (cb * _silu(zcol(7))).astype(BF16)


def _project(x, mod3, mod_row_fn, norm_w, w_in, qw_b, kw_b, rope_t, emit_cache, tm):
    b, n, d = x.shape
    d_in = w_in.shape[1]
    d_att = d_in // 8
    nt = n // tm
    use_rope = rope_t is not None

    in_specs = [
        pl.BlockSpec((1, tm, d), lambda i, t: (i, t, 0)),
        pl.BlockSpec((1, 1, d), lambda i, t: (mod_row_fn(i), 0, 0)),
        pl.BlockSpec((1, 1, d), lambda i, t: (mod_row_fn(i), 0, 1)),
        pl.BlockSpec((1, d), lambda i, t: (0, 0)),
        pl.BlockSpec((d, d_in), lambda i, t: (0, 0)),
        pl.BlockSpec((HEAD_DIM, tm), lambda i, t: (0, 0)),
        pl.BlockSpec((HEAD_DIM, tm), lambda i, t: (0, 0)),
    ]
    args = [x, mod3, mod3, norm_w, w_in, qw_b, kw_b]
    if use_rope:
        in_specs += [pl.BlockSpec((ROPE_FREQS, tm), lambda i, t: (0, t))] * 4
        args += list(rope_t)

    tok = lambda: pl.BlockSpec((1, tm, d_att), lambda i, t: (i, t, 0))
    out_specs = [
        pl.BlockSpec((1, 2 * d_att, tm), lambda i, t: (i, 0, t)),
        tok(),
        pl.BlockSpec((1, d_att, tm), lambda i, t: (i, 0, t)),
        tok(), tok(), tok(),
    ]
    out_shape = [
        jax.ShapeDtypeStruct((b, 2 * d_att, n), BF16),
        jax.ShapeDtypeStruct((b, n, d_att), BF16),
        jax.ShapeDtypeStruct((b, d_att, n), BF16),
        jax.ShapeDtypeStruct((b, n, d_att), BF16),
        jax.ShapeDtypeStruct((b, n, d_att), BF16),
        jax.ShapeDtypeStruct((b, n, d_att), BF16),
    ]
    if emit_cache:
        out_specs += [tok(), tok()]
        out_shape += [jax.ShapeDtypeStruct((b, n, d_att), F32)] * 2

    return pl.pallas_call(
        functools.partial(_proj_kernel, use_rope=use_rope, emit_cache=emit_cache),
        grid=(b, nt),
        in_specs=in_specs,
        out_specs=out_specs,
        out_shape=out_shape,
        compiler_params=pltpu.CompilerParams(
            dimension_semantics=("parallel", "parallel"),
            vmem_limit_bytes=VMEM_LIMIT),
        name="project_rope" if use_rope else "project_ctx",
    )(*args)


def _attn_kernel(*refs, n_new_chunks, ck, has_cache, first_is_seq_start, lam_init):
    it = iter(refs)
    lam_ref, qt_ref, k_ref, vt_ref = (next(it) for _ in range(4))
    if has_cache:
        kc_ref, vc_ref = next(it), next(it)
    (ga_ref, u_ref, up_ref, un_ref, gc_ref, x_ref, gate_ref, sw_ref, cw_ref, cb_ref,
     wo_ref, out_ref, acc_ref, cat_ref) = (next(it) for _ in range(14))

    tq = x_ref.shape[1]
    n_heads = qt_ref.shape[1] // (2 * V_DIM)
    d_att = n_heads * V_DIM
    lam = lam_ref[0:1, 0:1]

    for h in range(n_heads):
        hs = slice(h * V_DIM, (h + 1) * V_DIM)
        q_t = [qt_ref[0, (2 * h + m) * V_DIM:(2 * h + m + 1) * V_DIM, :] for m in range(2)]

        def update(k_chunk, vt_chunk, stats):
            new = []
            for m in range(2):
                m_old, l_old = stats[m]
                s = jnp.dot(k_chunk, q_t[m], preferred_element_type=F32)
                m_new = jnp.maximum(m_old, jnp.max(s, axis=0, keepdims=True))
                alpha = jnp.exp(m_old - m_new)
                p = jnp.exp(s - m_new)
                l_new = alpha * l_old + jnp.sum(p, axis=0, keepdims=True)
                pv = jnp.dot(vt_chunk, p.astype(BF16), preferred_element_type=F32)
                acc_ref[m] = alpha * acc_ref[m] + pv
                new.append((m_new, l_new))
            return tuple(new)

        acc_ref[...] = jnp.zeros_like(acc_ref)
        neg = jnp.full((1, tq), -jnp.inf, F32)
        zero = jnp.zeros((1, tq), F32)
        stats = ((neg, zero), (neg, zero))

        def body(j, stats):
            off = pl.multiple_of(j * ck, ck)
            return update(k_ref[0, pl.ds(off, ck), hs], vt_ref[0, hs, pl.ds(off, ck)], stats)

        stats = lax.fori_loop(0, n_new_chunks, body, stats)
        if has_cache:
            stats = update(kc_ref[0, :, hs], vc_ref[0, hs, :], stats)

        (_, l0), (_, l1) = stats
        o_t = acc_ref[0] * (1.0 / l0) - acc_ref[1] * (lam / l1)
        o = o_t.T
        ms = jnp.mean(o * o, axis=-1, keepdims=True)
        att = o * lax.rsqrt(ms + EPS) * sw_ref[...] * (1.0 - lam_init)
        cat_ref[:, hs] = (att * ga_ref[0, :, hs].astype(F32)).astype(BF16)

    u = u_ref[0].astype(F32)
    t = pl.program_id(1)
    prev_row = up_ref[0, BF16_SUBLANES - 1:BF16_SUBLANES, :].astype(F32)
    next_row = un_ref[0, 0:1, :].astype(F32)
    prev_row = jnp.where(t == 0, 0.0, prev_row)
    next_row = jnp.where(t == pl.num_programs(1) - 1, 0.0, next_row)
    rows = lax.broadcasted_iota(jnp.int32, u.shape, 0)
    u_prev = jnp.where(rows == 0, prev_row, pltpu.roll(u, 1, axis=0))
    u_next = jnp.where(rows == tq - 1, next_row, pltpu.roll(u, tq - 1, axis=0))
    conv = u_prev * cw_ref[0:1, :] + u * cw_ref[1:2, :] + u_next * cw_ref[2:3, :] + cb_ref[...]
    cat_ref[:, d_att:] = (gc_ref[0].astype(F32) * conv).astype(BF16)

    y = jnp.dot(cat_ref[...], wo_ref[...], preferred_element_type=F32)
    out_ref[0] = x_ref[0] + gate_ref[0] * y


def _attend(lam, qt, k, vt, cache, ga, u, gc, x, mod3, mod_row_fn, subln_w, conv_w, conv_b,
            w_out, lam_init, tq, ck):
    b, n, d = x.shape
    d_att = k.shape[2]
    nt = n // tq
    has_cache = cache is not None
    halo = BF16_SUBLANES
    blocks_per_tile = tq // halo
    n_halo_blocks = n // halo

    in_specs = [
        pl.BlockSpec((8, LANE), lambda i, t: (0, 0)),
        pl.BlockSpec((1, 2 * d_att, tq), lambda i, t: (i, 0, t)),
        pl.BlockSpec((1, n, d_att), lambda i, t: (i, 0, 0)),
        pl.BlockSpec((1, d_att, n), lambda i, t: (i, 0, 0)),
    ]
    args = [lam, qt, k, vt]
    if has_cache:
        kc, vct = cache
        past = kc.shape[1]
        in_specs += [
            pl.BlockSpec((1, past, d_att), lambda i, t: (i, 0, 0)),
            pl.BlockSpec((1, d_att, past), lambda i, t: (i, 0, 0)),
        ]
        args += [kc, vct]
    tok = lambda: pl.BlockSpec((1, tq, d_att), lambda i, t: (i, t, 0))
    in_specs += [
        tok(),
        tok(),
        pl.BlockSpec((1, halo, d_att),
                     lambda i, t: (i, jnp.maximum(t * blocks_per_tile - 1, 0), 0)),
        pl.BlockSpec((1, halo, d_att),
                     lambda i, t: (i, jnp.minimum((t + 1) * blocks_per_tile, n_halo_blocks - 1), 0)),
        tok(),
        pl.BlockSpec((1, tq, d), lambda i, t: (i, t, 0)),
        pl.BlockSpec((1, 1, d), lambda i, t: (mod_row_fn(i), 0, 2)),
        pl.BlockSpec((1, V_DIM), lambda i, t: (0, 0)),
        pl.BlockSpec((CONV_W, d_att), lambda i, t: (0, 0)),
        pl.BlockSpec((1, d_att), lambda i, t: (0, 0)),
        pl.BlockSpec((2 * d_att, d), lambda i, t: (0, 0)),
    ]
    args += [ga, u, u, u, gc, x, mod3, subln_w, conv_w, conv_b, w_out]

    return pl.pallas_call(
        functools.partial(_attn_kernel, n_new_chunks=n // ck, ck=ck, has_cache=has_cache,
                          first_is_seq_start=True, lam_init=lam_init),
        grid=(b, nt),
        in_specs=in_specs,
        out_specs=pl.BlockSpec((1, tq, d), lambda i, t: (i, t, 0)),
        out_shape=jax.ShapeDtypeStruct((b, n, d), F32),
        scratch_shapes=[
            pltpu.VMEM((2, V_DIM, tq), F32),
            pltpu.VMEM((tq, 2 * d_att), BF16),
        ],
        compiler_params=pltpu.CompilerParams(
            dimension_semantics=("parallel", "arbitrary"),
            vmem_limit_bytes=VMEM_LIMIT),
        name="attend_latent" if has_cache else "attend_ctx",
    )(*args)


def _rope_tables_t(n):
    t = jnp.arange(n)
    row = (t // GRID_W).astype(F32)
    col = (t % GRID_W).astype(F32)
    inv = 1.0 / (ROPE_BASE ** (jnp.arange(ROPE_FREQS, dtype=F32) / ROPE_FREQS))
    ang_r = row[:, None] * inv
    ang_c = col[:, None] * inv
    return (jnp.cos(ang_r).T, jnp.sin(ang_r).T, jnp.cos(ang_c).T, jnp.sin(ang_c).T)


def kernel(x_prompt, x_sample, cache_k, cache_v, c, c_ctx, norm_w, w_ada, b_ada, w_in,
           q_norm_w, k_norm_w, lambda_q1, lambda_k1, lambda_q2, lambda_k2, subln_w,
           conv_w, conv_b, w_out):
    depth = norm_w.shape[0]
    assert depth == 1
    i = 0
    lam_init = 0.8 - 0.6 * math.exp(-0.3 * i)
    b_ctx, n_ctx, d = x_prompt.shape
    b_lat, n_lat, _ = x_sample.shape
    d_att = w_in.shape[2] // 8
    tile = 256

    n_rows = 16
    cvecs = jnp.zeros((n_rows, d), F32).at[:b_lat].set(c).at[b_lat].set(c_ctx)
    mod, lam = _modulation(cvecs, w_ada[i], b_ada[i][None], lambda_q1[i][None], lambda_k1[i][None],
                           lambda_q2[i][None], lambda_k2[i][None], lam_init)
    mod3 = mod.reshape(n_rows, 1, 3 * d)

    w_in_b = w_in[i].astype(BF16)
    w_out_b = w_out[i].astype(BF16)
    nw = norm_w[i][None]
    qw_b = jnp.broadcast_to(q_norm_w[i][:, None], (HEAD_DIM, tile))
    kw_b = jnp.broadcast_to(k_norm_w[i][:, None], (HEAD_DIM, tile))
    sw = subln_w[i][None]
    cw = conv_w[i]
    cb = conv_b[i][None]

    ctx_row = lambda bi: b_lat
    lat_row = lambda bi: bi

    qt, k, vt, ga, u, gc, kf, vf = _project(x_prompt, mod3, ctx_row, nw, w_in_b, qw_b, kw_b,
                                            None, True, tile)
    y_prompt = _attend(lam, qt, k, vt, None, ga, u, gc, x_prompt, mod3, ctx_row, sw, cw, cb,
                       w_out_b, lam_init, tile, tile)
    n_heads = d_att // V_DIM
    new_cache_k = kf.reshape(b_ctx, 1, n_ctx, n_heads, 2, HEAD_DIM)
    new_cache_v = vf.reshape(b_ctx, 1, n_ctx, n_heads, V_DIM)

    rope_t = _rope_tables_t(n_lat)
    qt, k, vt, ga, u, gc = _project(x_sample, mod3, lat_row, nw, w_in_b, qw_b, kw_b,
                                    rope_t, False, tile)
    past = cache_k.shape[2]
    kc = cache_k[:, i].reshape(b_lat, past, d_att).astype(BF16)
    vct = jnp.swapaxes(cache_v[:, i].reshape(b_lat, past, d_att), 1, 2).astype(BF16)
    y_sample = _attend(lam, qt, k, vt, (kc, vct), ga, u, gc, x_sample, mod3, lat_row, sw, cw, cb,
                       w_out_b, lam_init, tile, tile)
    return (y_prompt, y_sample, new_cache_k, new_cache_v)
```

```python
import functools
import math

import jax
import jax.numpy as jnp
from jax import lax
from jax.experimental import pallas as pl
from jax.experimental.pallas import tpu as pltpu

F32 = jnp.float32
BF16 = jnp.bfloat16

HEAD_DIM = 64
V_DIM = 2 * HEAD_DIM
GRID_W = 64
ROPE_FREQS = HEAD_DIM // 4
ROPE_BASE = 10000.0
EPS = 1e-6
CONV_W = 3
LANE = 128
BF16_SUBLANES = 16
VMEM_LIMIT = 52 * 1024 * 1024
TILE = 256


def _silu(x):
    return x * (1.0 / (1.0 + jnp.exp(-x)))


def _mod_kernel(c_ref, w_ref, b_ref, lq1, lk1, lq2, lk2, mod_ref, lam_ref, *, lam_init):
    a = _silu(c_ref[...]).astype(BF16)
    m = jnp.dot(a, w_ref[...].astype(BF16), preferred_element_type=F32)
    mod_ref[...] = m + b_ref[...]
    e1 = jnp.exp(jnp.sum(lq1[...] * lk1[...], axis=-1, keepdims=True))
    e2 = jnp.exp(jnp.sum(lq2[...] * lk2[...], axis=-1, keepdims=True))
    lam_ref[...] = jnp.broadcast_to(e1 - e2 + lam_init, lam_ref.shape)


def _modulation(cvecs, w_ada, b_ada, lq1, lk1, lq2, lk2, lam_init):
    rows, d = cvecs.shape
    n_out = w_ada.shape[1]
    bn = 512
    vec = lambda: pl.BlockSpec((1, HEAD_DIM), lambda j: (0, 0))
    return pl.pallas_call(
        functools.partial(_mod_kernel, lam_init=lam_init),
        grid=(n_out // bn,),
        in_specs=[
            pl.BlockSpec((rows, d), lambda j: (0, 0)),
            pl.BlockSpec((d, bn), lambda j: (0, j)),
            pl.BlockSpec((1, bn), lambda j: (0, j)),
            vec(), vec(), vec(), vec(),
        ],
        out_specs=[
            pl.BlockSpec((rows, bn), lambda j: (0, j)),
            pl.BlockSpec((8, LANE), lambda j: (0, 0)),
        ],
        out_shape=[
            jax.ShapeDtypeStruct((rows, n_out), F32),
            jax.ShapeDtypeStruct((8, LANE), F32),
        ],
        name="modulation",
    )(cvecs, w_ada, b_ada, lq1, lk1, lq2, lk2)


def _norm_rope_t(z_t, w_b, rope):
    outs = []
    f = ROPE_FREQS
    for g in range(z_t.shape[0] // HEAD_DIM):
        zg = z_t[g * HEAD_DIM:(g + 1) * HEAD_DIM, :]
        ms = jnp.mean(zg * zg, axis=0, keepdims=True)
        yg = zg * lax.rsqrt(ms + EPS) * w_b
        if rope is not None:
            cr, sr, cc, sc = rope
            x1r, x2r, x1c, x2c = yg[0:f], yg[f:2 * f], yg[2 * f:3 * f], yg[3 * f:4 * f]
            yg = jnp.concatenate([x1r * cr - x2r * sr, x2r * cr + x1r * sr,
                                  x1c * cc - x2c * sc, x2c * cc + x1c * sc], axis=0)
        outs.append(yg)
    return outs


def _proj_kernel(*refs, use_rope, emit_cache, append_cache):
    it = iter(refs)
    x_ref, shift_ref, scale_ref, nw_ref, w_ref, qw_ref, kw_ref = (next(it) for _ in range(7))
    rope_refs = tuple(next(it) for _ in range(4)) if use_rope else None
    if append_cache:
        ck_ref, cv_ref = next(it), next(it)
    qt_ref, k_ref, vt_ref, ga_ref, u_ref, gc_ref = (next(it) for _ in range(6))
    if emit_cache:
        kf_ref, vf_ref = next(it), next(it)

    def project():
        x = x_ref[0]
        ms = jnp.mean(x * x, axis=-1, keepdims=True)
        y = x * lax.rsqrt(ms + EPS) * nw_ref[...]
        h = (y * (1.0 + scale_ref[0]) + shift_ref[0]).astype(BF16)
        rope = tuple(r[...] for r in rope_refs) if use_rope else None
        d_att = k_ref.shape[2]

        def zcol(i):
            return jnp.dot(h, w_ref[:, i * d_att:(i + 1) * d_att], preferred_element_type=F32)

        q_groups = _norm_rope_t(zcol(0).T, qw_ref[...], rope)
        zero = jnp.zeros_like(q_groups[0])
        for g, yg in enumerate(q_groups):
            yg = yg * (HEAD_DIM ** -0.5)
            pair = [yg, zero] if g % 2 == 0 else [zero, yg]
            qt_ref[0, g * V_DIM:(g + 1) * V_DIM, :] = jnp.concatenate(pair, axis=0).astype(BF16)

        k = jnp.concatenate(_norm_rope_t(zcol(1).T, kw_ref[...], rope), axis=0).T
        k_ref[0] = k.astype(BF16)
        v = zcol(2)
        vt_ref[0] = v.T.astype(BF16)
        if emit_cache:
            kf_ref[0] = k
            vf_ref[0] = v

        ga_ref[0] = _silu(zcol(3)).astype(BF16)
        cb = zcol(4)
        u_ref[0] = (zcol(5) * zcol(6)).astype(BF16)
        gc_ref[0] = (cb * _silu(zcol(7))).astype(BF16)

    if append_cache:
        last = pl.num_programs(1) - 1
        pl.when(pl.program_id(1) < last)(project)

        @pl.when(pl.program_id(1) == last)
        def _():
            k_ref[0] = ck_ref[0].astype(BF16)
            vt_ref[0] = cv_ref[0].T.astype(BF16)
    else:
        project()


def _project(x, mod3, mod_row_fn, norm_w, w_in, qw_b, kw_b, rope_t, emit_cache, cache):
    b, n, d = x.shape
    d_in = w_in.shape[1]
    d_att = d_in // 8
    tm = TILE
    nt = n // tm
    use_rope = rope_t is not None
    append_cache = cache is not None
    n_keys = n
    tok_t = lambda t: t
    if append_cache:
        assert cache[0].shape[1] == tm
        n_keys = n + tm
        tok_t = lambda t: jnp.minimum(t, nt - 1)

    in_specs = [
        pl.BlockSpec((1, tm, d), lambda i, t: (i, tok_t(t), 0)),
        pl.BlockSpec((1, 1, d), lambda i, t: (mod_row_fn(i), 0, 0)),
        pl.BlockSpec((1, 1, d), lambda i, t: (mod_row_fn(i), 0, 1)),
        pl.BlockSpec((1, d), lambda i, t: (0, 0)),
        pl.BlockSpec((d, d_in), lambda i, t: (0, 0)),
        pl.BlockSpec((HEAD_DIM, tm), lambda i, t: (0, 0)),
        pl.BlockSpec((HEAD_DIM, tm), lambda i, t: (0, 0)),
    ]
    args = [x, mod3, mod3, norm_w, w_in, qw_b, kw_b]
    if use_rope:
        in_specs += [pl.BlockSpec((ROPE_FREQS, tm), lambda i, t: (0, tok_t(t)))] * 4
        args += list(rope_t)
    if append_cache:
        in_specs += [pl.BlockSpec((1, tm, d_att), lambda i, t: (i, 0, 0))] * 2
        args += list(cache)

    tok = lambda: pl.BlockSpec((1, tm, d_att), lambda i, t: (i, tok_t(t), 0))
    out_specs = [
        pl.BlockSpec((1, 2 * d_att, tm), lambda i, t: (i, 0, tok_t(t))),
        pl.BlockSpec((1, tm, d_att), lambda i, t: (i, t, 0)),
        pl.BlockSpec((1, d_att, tm), lambda i, t: (i, 0, t)),
        tok(), tok(), tok(),
    ]
    out_shape = [
        jax.ShapeDtypeStruct((b, 2 * d_att, n), BF16),
        jax.ShapeDtypeStruct((b, n_keys, d_att), BF16),
        jax.ShapeDtypeStruct((b, d_att, n_keys), BF16),
        jax.ShapeDtypeStruct((b, n, d_att), BF16),
        jax.ShapeDtypeStruct((b, n, d_att), BF16),
        jax.ShapeDtypeStruct((b, n, d_att), BF16),
    ]
    if emit_cache:
        out_specs += [tok(), tok()]
        out_shape += [jax.ShapeDtypeStruct((b, n, d_att), F32)] * 2

    return pl.pallas_call(
        functools.partial(_proj_kernel, use_rope=use_rope, emit_cache=emit_cache,
                          append_cache=append_cache),
        grid=(b, n_keys // tm),
        in_specs=in_specs,
        out_specs=out_specs,
        out_shape=out_shape,
        compiler_params=pltpu.CompilerParams(
            dimension_semantics=("parallel", "arbitrary"),
            vmem_limit_bytes=VMEM_LIMIT),
        name="project_latent" if use_rope else "project_ctx",
    )(*args)


def _attn_kernel(lam_ref, qt_ref, k_ref, vt_ref, ga_ref, u_ref, up_ref, un_ref, gc_ref, x_ref,
                 gate_ref, sw_ref, cw_ref, cb_ref, wo_ref, out_ref,
                 s_a, s_b, p_a, p_b, al_a, al_b, st_ref, acc_ref, cat_ref, *, n_chunks, lam_init):
    tq = x_ref.shape[1]
    ck = TILE
    n_chains = qt_ref.shape[1] // V_DIM
    n_heads = n_chains // 2
    d_att = n_heads * V_DIM
    lam = lam_ref[0:1, 0:1]

    def head_slice(i):
        return slice((i // 2) * V_DIM, (i // 2 + 1) * V_DIM)

    def key_offset(c):
        return c * ck if isinstance(c, int) else pl.multiple_of(c * ck, ck)

    def scores(c, s_ref):
        off = key_offset(c)
        for i in range(n_chains):
            k_chunk = k_ref[0, pl.ds(off, ck), head_slice(i)]
            s_ref[i] = jnp.dot(k_chunk, qt_ref[0, i * V_DIM:(i + 1) * V_DIM, :],
                               preferred_element_type=F32)

    def softmax_step(s_ref, p_ref, al_ref):
        for i in range(n_chains):
            s = s_ref[i]
            m_old = st_ref[0, i:i + 1, :]
            l_old = st_ref[1, i:i + 1, :]
            m_new = jnp.maximum(m_old, jnp.max(s, axis=0, keepdims=True))
            alpha = jnp.exp(m_old - m_new)
            p = jnp.exp(s - m_new)
            st_ref[0, i:i + 1, :] = m_new
            st_ref[1, i:i + 1, :] = alpha * l_old + jnp.sum(p, axis=0, keepdims=True)
            al_ref[i:i + 1, :] = alpha
            p_ref[i] = p.astype(BF16)

    def accumulate(c, p_ref, al_ref, first=False):
        off = key_offset(c)
        for i in range(n_chains):
            pv = jnp.dot(vt_ref[0, head_slice(i), pl.ds(off, ck)], p_ref[i],
                         preferred_element_type=F32)
            acc_ref[i] = pv if first else al_ref[i:i + 1, :] * acc_ref[i] + pv

    st_ref[0] = jnp.full(st_ref.shape[1:], -jnp.inf, F32)
    st_ref[1] = jnp.zeros(st_ref.shape[1:], F32)

    if n_chunks == 1:
        scores(0, s_a)
        softmax_step(s_a, p_a, al_a)
        accumulate(0, p_a, al_a, first=True)
    else:
        assert n_chunks % 2 == 1 and n_chunks >= 5
        scores(0, s_a)
        scores(1, s_b)
        softmax_step(s_a, p_a, al_a)
        scores(2, s_a)
        softmax_step(s_b, p_b, al_b)
        accumulate(0, p_a, al_a, first=True)

        def pair(jj, carry):
            c = 2 * jj
            scores(c + 1, s_b)
            softmax_step(s_a, p_a, al_a)
            accumulate(c - 1, p_b, al_b)
            scores(c + 2, s_a)
            softmax_step(s_b, p_b, al_b)
            accumulate(c, p_a, al_a)
            return carry

        lax.fori_loop(1, (n_chunks - 1) // 2, pair, 0)
        last = n_chunks - 1
        softmax_step(s_a, p_a, al_a)
        accumulate(last - 1, p_b, al_b)
        accumulate(last, p_a, al_a)

    for h in range(n_heads):
        hs = slice(h * V_DIM, (h + 1) * V_DIM)
        l0 = st_ref[1, 2 * h:2 * h + 1, :]
        l1 = st_ref[1, 2 * h + 1:2 * h + 2, :]
        o_t = acc_ref[2 * h] * (1.0 / l0) - acc_ref[2 * h + 1] * (lam / l1)
        o = o_t.T
        ms = jnp.mean(o * o, axis=-1, keepdims=True)
        att = o * lax.rsqrt(ms + EPS) * sw_ref[...] * (1.0 - lam_init)
        cat_ref[:, hs] = (att * ga_ref[0, :, hs].astype(F32)).astype(BF16)

    u = u_ref[0].astype(F32)
    t = pl.program_id(1)
    prev_row = up_ref[0].astype(F32)[BF16_SUBLANES - 1:BF16_SUBLANES, :]
    next_row = un_ref[0].astype(F32)[0:1, :]
    prev_row = jnp.where(t == 0, 0.0, prev_row)
    next_row = jnp.where(t == pl.num_programs(1) - 1, 0.0, next_row)
    rows = lax.broadcasted_iota(jnp.int32, u.shape, 0)
    u_prev = jnp.where(rows == 0, prev_row, pltpu.roll(u, 1, axis=0))
    u_next = jnp.where(rows == tq - 1, next_row, pltpu.roll(u, tq - 1, axis=0))
    conv = u_prev * cw_ref[0:1, :] + u * cw_ref[1:2, :] + u_next * cw_ref[2:3, :] + cb_ref[...]
    cat_ref[:, d_att:] = (gc_ref[0].astype(F32) * conv).astype(BF16)

    y = jnp.dot(cat_ref[...], wo_ref[...], preferred_element_type=F32)
    out_ref[0] = x_ref[0] + gate_ref[0] * y


def _attend(lam, qt, k, vt, ga, u, gc, x, mod3, mod_row_fn, subln_w, conv_w, conv_b, w_out,
            lam_init):
    b, n, d = x.shape
    n_keys, d_att = k.shape[1], k.shape[2]
    tq = TILE
    nt = n // tq
    halo = BF16_SUBLANES
    blocks_per_tile = tq // halo
    n_halo_blocks = n // halo
    n_chains = 2 * d_att // V_DIM

    tok = lambda: pl.BlockSpec((1, tq, d_att), lambda i, t: (i, t, 0))
    in_specs = [
        pl.BlockSpec((8, LANE), lambda i, t: (0, 0)),
        pl.BlockSpec((1, 2 * d_att, tq), lambda i, t: (i, 0, t)),
        pl.BlockSpec((1, n_keys, d_att), lambda i, t: (i, 0, 0)),
        pl.BlockSpec((1, d_att, n_keys), lambda i, t: (i, 0, 0)),
        tok(),
        tok(),
        pl.BlockSpec((1, halo, d_att),
                     lambda i, t: (i, jnp.maximum(t * blocks_per_tile - 1, 0), 0)),
        pl.BlockSpec((1, halo, d_att),
                     lambda i, t: (i, jnp.minimum((t + 1) * blocks_per_tile, n_halo_blocks - 1), 0)),
        tok(),
        pl.BlockSpec((1, tq, d), lambda i, t: (i, t, 0)),
        pl.BlockSpec((1, 1, d), lambda i, t: (mod_row_fn(i), 0, 2)),
        pl.BlockSpec((1, V_DIM), lambda i, t: (0, 0)),
        pl.BlockSpec((CONV_W, d_att), lambda i, t: (0, 0)),
        pl.BlockSpec((1, d_att), lambda i, t: (0, 0)),
        pl.BlockSpec((2 * d_att, d), lambda i, t: (0, 0)),
    ]
    args = [lam, qt, k, vt, ga, u, u, u, gc, x, mod3, subln_w, conv_w, conv_b, w_out]
    score_buf = lambda: pltpu.VMEM((n_chains, TILE, tq), F32)
    prob_buf = lambda: pltpu.VMEM((n_chains, TILE, tq), BF16)
    alpha_buf = lambda: pltpu.VMEM((n_chains, tq), F32)

    return pl.pallas_call(
        functools.partial(_attn_kernel, n_chunks=n_keys // TILE, lam_init=lam_init),
        grid=(b, nt),
        in_specs=in_specs,
        out_specs=pl.BlockSpec((1, tq, d), lambda i, t: (i, t, 0)),
        out_shape=jax.ShapeDtypeStruct((b, n, d), F32),
        scratch_shapes=[
            score_buf(), score_buf(), prob_buf(), prob_buf(), alpha_buf(), alpha_buf(),
            pltpu.VMEM((2, n_chains, tq), F32),
            pltpu.VMEM((n_chains, V_DIM, tq), F32),
            pltpu.VMEM((tq, 2 * d_att), BF16),
        ],
        compiler_params=pltpu.CompilerParams(
            dimension_semantics=("parallel", "arbitrary"),
            vmem_limit_bytes=VMEM_LIMIT),
        name="attend_latent" if n_keys > n else "attend_ctx",
    )(*args)


def _rope_tables_t(n):
    t = jnp.arange(n)
    row = (t // GRID_W).astype(F32)
    col = (t % GRID_W).astype(F32)
    inv = 1.0 / (ROPE_BASE ** (jnp.arange(ROPE_FREQS, dtype=F32) / ROPE_FREQS))
    ang_r = row[:, None] * inv
    ang_c = col[:, None] * inv
    return (jnp.cos(ang_r).T, jnp.sin(ang_r).T, jnp.cos(ang_c).T, jnp.sin(ang_c).T)


def kernel(x_prompt, x_sample, cache_k, cache_v, c, c_ctx, norm_w, w_ada, b_ada, w_in,
           q_norm_w, k_norm_w, lambda_q1, lambda_k1, lambda_q2, lambda_k2, subln_w,
           conv_w, conv_b, w_out):
    depth = norm_w.shape[0]
    assert depth == 1
    i = 0
    lam_init = 0.8 - 0.6 * math.exp(-0.3 * i)
    b_ctx, n_ctx, d = x_prompt.shape
    b_lat, n_lat, _ = x_sample.shape
    d_att = w_in.shape[2] // 8
    n_heads = d_att // V_DIM

    n_rows = 16
    cvecs = jnp.zeros((n_rows, d), F32).at[:b_lat].set(c).at[b_lat].set(c_ctx)
    mod, lam = _modulation(cvecs, w_ada[i], b_ada[i][None], lambda_q1[i][None], lambda_k1[i][None],
                           lambda_q2[i][None], lambda_k2[i][None], lam_init)
    mod3 = mod.reshape(n_rows, 1, 3 * d)

    w_in_b = w_in[i].astype(BF16)
    w_out_b = w_out[i].astype(BF16)
    nw = norm_w[i][None]
    qw_b = jnp.broadcast_to(q_norm_w[i][:, None], (HEAD_DIM, TILE))
    kw_b = jnp.broadcast_to(k_norm_w[i][:, None], (HEAD_DIM, TILE))
    sw = subln_w[i][None]
    cw = conv_w[i]
    cb = conv_b[i][None]

    ctx_row = lambda bi: b_lat
    lat_row = lambda bi: bi

    qt, k, vt, ga, u, gc, kf, vf = _project(x_prompt, mod3, ctx_row, nw, w_in_b, qw_b, kw_b,
                                            None, True, None)
    y_prompt = _attend(lam, qt, k, vt, ga, u, gc, x_prompt, mod3, ctx_row, sw, cw, cb, w_out_b,
                       lam_init)
    new_cache_k = kf.reshape(b_ctx, 1, n_ctx, n_heads, 2, HEAD_DIM)
    new_cache_v = vf.reshape(b_ctx, 1, n_ctx, n_heads, V_DIM)

    past = cache_k.shape[2]
    cache = (cache_k[:, i].reshape(b_lat, past, d_att), cache_v[:, i].reshape(b_lat, past, d_att))
    qt, k, vt, ga, u, gc = _project(x_sample, mod3, lat_row, nw, w_in_b, qw_b, kw_b,
                                    _rope_tables_t(n_lat), False, cache)
    y_sample = _attend(lam, qt, k, vt, ga, u, gc, x_sample, mod3, lat_row, sw, cw, cb, w_out_b,
                       lam_init)
    return (y_prompt, y_sample, new_cache_k, new_cache_v)
```

```python
import functools
import math

import jax
import jax.numpy as jnp
from jax import lax
from jax.experimental import pallas as pl
from jax.experimental.pallas import tpu as pltpu

F32 = jnp.float32
BF16 = jnp.bfloat16

HEAD_DIM = 64
V_DIM = 2 * HEAD_DIM
GRID_W = 64
ROPE_FREQS = HEAD_DIM // 4
ROPE_BASE = 10000.0
EPS = 1e-6
CONV_W = 3
LANE = 128
BF16_SUBLANES = 16
VMEM_LIMIT = 52 * 1024 * 1024
TILE = 256
Q_SCALE = HEAD_DIM ** -0.5 * math.log2(math.e)
V_EXT = V_DIM + BF16_SUBLANES


def _silu(x):
    return x * (1.0 / (1.0 + jnp.exp(-x)))


def _mod_kernel(c_ref, w_ref, b_ref, lq1, lk1, lq2, lk2, mod_ref, lam_ref, *, lam_init):
    a = _silu(c_ref[...]).astype(BF16)
    m = jnp.dot(a, w_ref[...].astype(BF16), preferred_element_type=F32)
    mod_ref[...] = m + b_ref[...]
    e1 = jnp.exp(jnp.sum(lq1[...] * lk1[...], axis=-1, keepdims=True))
    e2 = jnp.exp(jnp.sum(lq2[...] * lk2[...], axis=-1, keepdims=True))
    lam_ref[...] = jnp.broadcast_to(e1 - e2 + lam_init, lam_ref.shape)


def _modulation(cvecs, w_ada, b_ada, lq1, lk1, lq2, lk2, lam_init):
    rows, d = cvecs.shape
    n_out = w_ada.shape[1]
    bn = 512
    vec = lambda: pl.BlockSpec((1, HEAD_DIM), lambda j: (0, 0))
    return pl.pallas_call(
        functools.partial(_mod_kernel, lam_init=lam_init),
        grid=(n_out // bn,),
        in_specs=[
            pl.BlockSpec((rows, d), lambda j: (0, 0)),
            pl.BlockSpec((d, bn), lambda j: (0, j)),
            pl.BlockSpec((1, bn), lambda j: (0, j)),
            vec(), vec(), vec(), vec(),
        ],
        out_specs=[
            pl.BlockSpec((rows, bn), lambda j: (0, j)),
            pl.BlockSpec((8, LANE), lambda j: (0, 0)),
        ],
        out_shape=[
            jax.ShapeDtypeStruct((rows, n_out), F32),
            jax.ShapeDtypeStruct((8, LANE), F32),
        ],
        name="modulation",
    )(cvecs, w_ada, b_ada, lq1, lk1, lq2, lk2)


def _norm_rope_t(z_t, w_b, rope):
    outs = []
    f = ROPE_FREQS
    for g in range(z_t.shape[0] // HEAD_DIM):
        zg = z_t[g * HEAD_DIM:(g + 1) * HEAD_DIM, :]
        ms = jnp.mean(zg * zg, axis=0, keepdims=True)
        yg = zg * lax.rsqrt(ms + EPS) * w_b
        if rope is not None:
            cr, sr, cc, sc = rope
            x1r, x2r, x1c, x2c = yg[0:f], yg[f:2 * f], yg[2 * f:3 * f], yg[3 * f:4 * f]
            yg = jnp.concatenate([x1r * cr - x2r * sr, x2r * cr + x1r * sr,
                                  x1c * cc - x2c * sc, x2c * cc + x1c * sc], axis=0)
        outs.append(yg)
    return outs


def _store_vt(vt_ref, v_t):
    t = v_t.shape[1]
    ones_blk = (lax.broadcasted_iota(jnp.int32, (V_EXT - V_DIM, t), 0) == 0).astype(BF16)
    for h in range(v_t.shape[0] // V_DIM):
        vt_ref[0, h * V_EXT:h * V_EXT + V_DIM, :] = v_t[h * V_DIM:(h + 1) * V_DIM].astype(BF16)
        vt_ref[0, h * V_EXT + V_DIM:(h + 1) * V_EXT, :] = ones_blk


def _proj_kernel(*refs, use_rope, emit_cache, append_cache):
    it = iter(refs)
    x_ref, shift_ref, scale_ref, nw_ref, w_ref, qw_ref, kw_ref = (next(it) for _ in range(7))
    rope_refs = tuple(next(it) for _ in range(4)) if use_rope else None
    if append_cache:
        ck_ref, cv_ref = next(it), next(it)
    qt_ref, k_ref, vt_ref, ga_ref, u_ref, gc_ref = (next(it) for _ in range(6))
    if emit_cache:
        kf_ref, vf_ref = next(it), next(it)

    def project():
        x = x_ref[0]
        ms = jnp.mean(x * x, axis=-1, keepdims=True)
        y = x * lax.rsqrt(ms + EPS) * nw_ref[...]
        h = (y * (1.0 + scale_ref[0]) + shift_ref[0]).astype(BF16)
        rope = tuple(r[...] for r in rope_refs) if use_rope else None
        d_att = k_ref.shape[2]

        def zcol(i):
            return jnp.dot(h, w_ref[:, i * d_att:(i + 1) * d_att], preferred_element_type=F32)

        q_groups = _norm_rope_t(zcol(0).T, qw_ref[...], rope)
        zero = jnp.zeros_like(q_groups[0])
        for g, yg in enumerate(q_groups):
            yg = yg * Q_SCALE
            pair = [yg, zero] if g % 2 == 0 else [zero, yg]
            qt_ref[0, g * V_DIM:(g + 1) * V_DIM, :] = jnp.concatenate(pair, axis=0).astype(BF16)

        k = jnp.concatenate(_norm_rope_t(zcol(1).T, kw_ref[...], rope), axis=0).T
        k_ref[0] = k.astype(BF16)
        v = zcol(2)
        _store_vt(vt_ref, v.T)
        if emit_cache:
            kf_ref[0] = k
            vf_ref[0] = v

        ga_ref[0] = _silu(zcol(3)).astype(BF16)
        cb = zcol(4)
        u_ref[0] = (zcol(5) * zcol(6)).astype(BF16)
        gc_ref[0] = (cb * _silu(zcol(7))).astype(BF16)

    if append_cache:
        last = pl.num_programs(1) - 1
        pl.when(pl.program_id(1) < last)(project)

        @pl.when(pl.program_id(1) == last)
        def _():
            k_ref[0] = ck_ref[0].astype(BF16)
            _store_vt(vt_ref, cv_ref[0].T)
    else:
        project()


def _project(x, mod3, mod_row_fn, norm_w, w_in, qw_b, kw_b, rope_t, emit_cache, cache):
    b, n, d = x.shape
    d_in = w_in.shape[1]
    d_att = d_in // 8
    d_vt = d_att // V_DIM * V_EXT
    tm = TILE
    nt = n // tm
    use_rope = rope_t is not None
    append_cache = cache is not None
    n_keys = n
    tok_t = lambda t: t
    if append_cache:
        assert cache[0].shape[1] == tm
        n_keys = n + tm
        tok_t = lambda t: jnp.minimum(t, nt - 1)

    in_specs = [
        pl.BlockSpec((1, tm, d), lambda i, t: (i, tok_t(t), 0)),
        pl.BlockSpec((1, 1, d), lambda i, t: (mod_row_fn(i), 0, 0)),
        pl.BlockSpec((1, 1, d), lambda i, t: (mod_row_fn(i), 0, 1)),
        pl.BlockSpec((1, d), lambda i, t: (0, 0)),
        pl.BlockSpec((d, d_in), lambda i, t: (0, 0)),
        pl.BlockSpec((HEAD_DIM, tm), lambda i, t: (0, 0)),
        pl.BlockSpec((HEAD_DIM, tm), lambda i, t: (0, 0)),
    ]
    args = [x, mod3, mod3, norm_w, w_in, qw_b, kw_b]
    if use_rope:
        in_specs += [pl.BlockSpec((ROPE_FREQS, tm), lambda i, t: (0, tok_t(t)))] * 4
        args += list(rope_t)
    if append_cache:
        in_specs += [pl.BlockSpec((1, tm, d_att), lambda i, t: (i, 0, 0))] * 2
        args += list(cache)

    tok = lambda: pl.BlockSpec((1, tm, d_att), lambda i, t: (i, tok_t(t), 0))
    out_specs = [
        pl.BlockSpec((1, 2 * d_att, tm), lambda i, t: (i, 0, tok_t(t))),
        pl.BlockSpec((1, tm, d_att), lambda i, t: (i, t, 0)),
        pl.BlockSpec((1, d_vt, tm), lambda i, t: (i, 0, t)),
        tok(), tok(), tok(),
    ]
    out_shape = [
        jax.ShapeDtypeStruct((b, 2 * d_att, n), BF16),
        jax.ShapeDtypeStruct((b, n_keys, d_att), BF16),
        jax.ShapeDtypeStruct((b, d_vt, n_keys), BF16),
        jax.ShapeDtypeStruct((b, n, d_att), BF16),
        jax.ShapeDtypeStruct((b, n, d_att), BF16),
        jax.ShapeDtypeStruct((b, n, d_att), BF16),
    ]
    if emit_cache:
        out_specs += [tok(), tok()]
        out_shape += [jax.ShapeDtypeStruct((b, n, d_att), F32)] * 2

    return pl.pallas_call(
        functools.partial(_proj_kernel, use_rope=use_rope, emit_cache=emit_cache,
                          append_cache=append_cache),
        grid=(b, n_keys // tm),
        in_specs=in_specs,
        out_specs=out_specs,
        out_shape=out_shape,
        compiler_params=pltpu.CompilerParams(
            dimension_semantics=("parallel", "arbitrary"),
            vmem_limit_bytes=VMEM_LIMIT),
        name="project_latent" if use_rope else "project_ctx",
    )(*args)


def _attn_kernel(lam_ref, qt_ref, k_ref, vt_ref, ga_ref, u_ref, up_ref, un_ref, gc_ref, x_ref,
                 gate_ref, sw_ref, cw_ref, cb_ref, wo_ref, out_ref,
                 s_a, s_b, p_a, p_b, al_a, al_b, st_ref, acc_ref, cat_ref, *, n_chunks, lam_init):
    tq = x_ref.shape[1]
    ck = TILE
    n_chains = qt_ref.shape[1] // V_DIM
    n_heads = n_chains // 2
    d_att = n_heads * V_DIM
    lam = lam_ref[0:1, 0:1]

    def head_slice(i):
        return slice((i // 2) * V_DIM, (i // 2 + 1) * V_DIM)

    def key_offset(c):
        return c * ck if isinstance(c, int) else pl.multiple_of(c * ck, ck)

    def scores(c, s_ref):
        off = key_offset(c)
        for i in range(n_chains):
            k_chunk = k_ref[0, pl.ds(off, ck), head_slice(i)]
            s_ref[i] = jnp.dot(k_chunk, qt_ref[0, i * V_DIM:(i + 1) * V_DIM, :],
                               preferred_element_type=F32)

    def softmax_step(s_ref, p_ref, al_ref):
        for i in range(n_chains):
            s = s_ref[i]
            m_old = st_ref[i:i + 1, :]
            m_new = jnp.maximum(m_old, jnp.max(s, axis=0, keepdims=True))
            st_ref[i:i + 1, :] = m_new
            al_ref[i:i + 1, :] = jnp.exp2(m_old - m_new)
            p_ref[i] = jnp.exp2(s - m_new).astype(BF16)

    def accumulate(c, p_ref, al_ref, first=False):
        off = key_offset(c)
        for i in range(n_chains):
            h = i // 2
            pv = jnp.dot(vt_ref[0, h * V_EXT:(h + 1) * V_EXT, pl.ds(off, ck)], p_ref[i],
                         preferred_element_type=F32)
            acc_ref[i] = pv if first else al_ref[i:i + 1, :] * acc_ref[i] + pv

    st_ref[...] = jnp.full(st_ref.shape, -jnp.inf, F32)

    if n_chunks == 1:
        scores(0, s_a)
        softmax_step(s_a, p_a, al_a)
        accumulate(0, p_a, al_a, first=True)
    else:
        assert n_chunks % 2 == 1 and n_chunks >= 5
        scores(0, s_a)
        scores(1, s_b)
        softmax_step(s_a, p_a, al_a)
        scores(2, s_a)
        softmax_step(s_b, p_b, al_b)
        accumulate(0, p_a, al_a, first=True)

        def pair(jj, carry):
            c = 2 * jj
            scores(c + 1, s_b)
            softmax_step(s_a, p_a, al_a)
            accumulate(c - 1, p_b, al_b)
            scores(c + 2, s_a)
            softmax_step(s_b, p_b, al_b)
            accumulate(c, p_a, al_a)
            return carry

        lax.fori_loop(1, (n_chunks - 1) // 2, pair, 0)
        last = n_chunks - 1
        softmax_step(s_a, p_a, al_a)
        accumulate(last - 1, p_b, al_b)
        accumulate(last, p_a, al_a)

    for h in range(n_heads):
        hs = slice(h * V_DIM, (h + 1) * V_DIM)
        l0 = acc_ref[2 * h, V_DIM:V_DIM + 1, :]
        l1 = acc_ref[2 * h + 1, V_DIM:V_DIM + 1, :]
        o_t = (acc_ref[2 * h, 0:V_DIM, :] * (1.0 / l0)
               - acc_ref[2 * h + 1, 0:V_DIM, :] * (lam / l1))
        o = o_t.T
        ms = jnp.mean(o * o, axis=-1, keepdims=True)
        att = o * lax.rsqrt(ms + EPS) * sw_ref[...] * (1.0 - lam_init)
        cat_ref[:, hs] = (att * ga_ref[0, :, hs].astype(F32)).astype(BF16)

    u = u_ref[0].astype(F32)
    t = pl.program_id(1)
    prev_row = up_ref[0].astype(F32)[BF16_SUBLANES - 1:BF16_SUBLANES, :]
    next_row = un_ref[0].astype(F32)[0:1, :]
    prev_row = jnp.where(t == 0, 0.0, prev_row)
    next_row = jnp.where(t == pl.num_programs(1) - 1, 0.0, next_row)
    rows = lax.broadcasted_iota(jnp.int32, u.shape, 0)
    u_prev = jnp.where(rows == 0, prev_row, pltpu.roll(u, 1, axis=0))
    u_next = jnp.where(rows == tq - 1, next_row, pltpu.roll(u, tq - 1, axis=0))
    conv = u_prev * cw_ref[0:1, :] + u * cw_ref[1:2, :] + u_next * cw_ref[2:3, :] + cb_ref[...]
    cat_ref[:, d_att:] = (gc_ref[0].astype(F32) * conv).astype(BF16)

    y = jnp.dot(cat_ref[...], wo_ref[...], preferred_element_type=F32)
    out_ref[0] = x_ref[0] + gate_ref[0] * y


def _attend(lam, qt, k, vt, ga, u, gc, x, mod3, mod_row_fn, subln_w, conv_w, conv_b, w_out,
            lam_init):
    b, n, d = x.shape
    n_keys, d_att = k.shape[1], k.shape[2]
    tq = TILE
    nt = n // tq
    halo = BF16_SUBLANES
    blocks_per_tile = tq // halo
    n_halo_blocks = n // halo
    n_chains = 2 * d_att // V_DIM

    tok = lambda: pl.BlockSpec((1, tq, d_att), lambda i, t: (i, t, 0))
    in_specs = [
        pl.BlockSpec((8, LANE), lambda i, t: (0, 0)),
        pl.BlockSpec((1, 2 * d_att, tq), lambda i, t: (i, 0, t)),
        pl.BlockSpec((1, n_keys, d_att), lambda i, t: (i, 0, 0)),
        pl.BlockSpec((1, vt.shape[1], n_keys), lambda i, t: (i, 0, 0)),
        tok(),
        tok(),
        pl.BlockSpec((1, halo, d_att),
                     lambda i, t: (i, jnp.maximum(t * blocks_per_tile - 1, 0), 0)),
        pl.BlockSpec((1, halo, d_att),
                     lambda i, t: (i, jnp.minimum((t + 1) * blocks_per_tile, n_halo_blocks - 1), 0)),
        tok(),
        pl.BlockSpec((1, tq, d), lambda i, t: (i, t, 0)),
        pl.BlockSpec((1, 1, d), lambda i, t: (mod_row_fn(i), 0, 2)),
        pl.BlockSpec((1, V_DIM), lambda i, t: (0, 0)),
        pl.BlockSpec((CONV_W, d_att), lambda i, t: (0, 0)),
        pl.BlockSpec((1, d_att), lambda i, t: (0, 0)),
        pl.BlockSpec((2 * d_att, d), lambda i, t: (0, 0)),
    ]
    args = [lam, qt, k, vt, ga, u, u, u, gc, x, mod3, subln_w, conv_w, conv_b, w_out]
    score_buf = lambda: pltpu.VMEM((n_chains, TILE, tq), F32)
    prob_buf = lambda: pltpu.VMEM((n_chains, TILE, tq), BF16)
    alpha_buf = lambda: pltpu.VMEM((n_chains, tq), F32)

    return pl.pallas_call(
        functools.partial(_attn_kernel, n_chunks=n_keys // TILE, lam_init=lam_init),
        grid=(b, nt),
        in_specs=in_specs,
        out_specs=pl.BlockSpec((1, tq, d), lambda i, t: (i, t, 0)),
        out_shape=jax.ShapeDtypeStruct((b, n, d), F32),
        scratch_shapes=[
            score_buf(), score_buf(), prob_buf(), prob_buf(), alpha_buf(), alpha_buf(),
            pltpu.VMEM((n_chains, tq), F32),
            pltpu.VMEM((n_chains, V_EXT, tq), F32),
            pltpu.VMEM((tq, 2 * d_att), BF16),
        ],
        compiler_params=pltpu.CompilerParams(
            dimension_semantics=("parallel", "arbitrary"),
            vmem_limit_bytes=VMEM_LIMIT),
        name="attend_latent" if n_keys > n else "attend_ctx",
    )(*args)


def _rope_tables_t(n):
    t = jnp.arange(n)
    row = (t // GRID_W).astype(F32)
    col = (t % GRID_W).astype(F32)
    inv = 1.0 / (ROPE_BASE ** (jnp.arange(ROPE_FREQS, dtype=F32) / ROPE_FREQS))
    ang_r = row[:, None] * inv
    ang_c = col[:, None] * inv
    return (jnp.cos(ang_r).T, jnp.sin(ang_r).T, jnp.cos(ang_c).T, jnp.sin(ang_c).T)


def kernel(x_prompt, x_sample, cache_k, cache_v, c, c_ctx, norm_w, w_ada, b_ada, w_in,
           q_norm_w, k_norm_w, lambda_q1, lambda_k1, lambda_q2, lambda_k2, subln_w,
           conv_w, conv_b, w_out):
    depth = norm_w.shape[0]
    assert depth == 1
    i = 0
    lam_init = 0.8 - 0.6 * math.exp(-0.3 * i)
    b_ctx, n_ctx, d = x_prompt.shape
    b_lat, n_lat, _ = x_sample.shape
    d_att = w_in.shape[2] // 8
    n_heads = d_att // V_DIM

    n_rows = 16
    cvecs = jnp.zeros((n_rows, d), F32).at[:b_lat].set(c).at[b_lat].set(c_ctx)
    mod, lam = _modulation(cvecs, w_ada[i], b_ada[i][None], lambda_q1[i][None], lambda_k1[i][None],
                           lambda_q2[i][None], lambda_k2[i][None], lam_init)
    mod3 = mod.reshape(n_rows, 1, 3 * d)

    w_in_b = w_in[i].astype(BF16)
    w_out_b = w_out[i].astype(BF16)
    nw = norm_w[i][None]
    qw_b = jnp.broadcast_to(q_norm_w[i][:, None], (HEAD_DIM, TILE))
    kw_b = jnp.broadcast_to(k_norm_w[i][:, None], (HEAD_DIM, TILE))
    sw = subln_w[i][None]
    cw = conv_w[i]
    cb = conv_b[i][None]

    ctx_row = lambda bi: b_lat
    lat_row = lambda bi: bi

    qt, k, vt, ga, u, gc, kf, vf = _project(x_prompt, mod3, ctx_row, nw, w_in_b, qw_b, kw_b,
                                            None, True, None)
    y_prompt = _attend(lam, qt, k, vt, ga, u, gc, x_prompt, mod3, ctx_row, sw, cw, cb, w_out_b,
                       lam_init)
    new_cache_k = kf.reshape(b_ctx, 1, n_ctx, n_heads, 2, HEAD_DIM)
    new_cache_v = vf.reshape(b_ctx, 1, n_ctx, n_heads, V_DIM)

    past = cache_k.shape[2]
    cache = (cache_k[:, i].reshape(b_lat, past, d_att), cache_v[:, i].reshape(b_lat, past, d_att))
    qt, k, vt, ga, u, gc = _project(x_sample, mod3, lat_row, nw, w_in_b, qw_b, kw_b,
                                    _rope_tables_t(n_lat), False, cache)
    y_sample = _attend(lam, qt, k, vt, ga, u, gc, x_sample, mod3, lat_row, sw, cw, cb, w_out_b,
                       lam_init)
    return (y_prompt, y_sample, new_cache_k, new_cache_v)
```

```python
import functools
import math

import jax
import jax.numpy as jnp
from jax import lax
from jax.experimental import pallas as pl
from jax.experimental.pallas import tpu as pltpu

F32 = jnp.float32
BF16 = jnp.bfloat16

HEAD_DIM = 64
V_DIM = 2 * HEAD_DIM
GRID_W = 64
ROPE_FREQS = HEAD_DIM // 4
ROPE_BASE = 10000.0
EPS = 1e-6
CONV_W = 3
LANE = 128
BF16_SUBLANES = 16
VMEM_LIMIT = 52 * 1024 * 1024
TILE = 256
KEY_UNIT = TILE
Q_SCALE = HEAD_DIM ** -0.5 * math.log2(math.e)
V_EXT = V_DIM + BF16_SUBLANES


def _silu(x):
    return x * (1.0 / (1.0 + jnp.exp(-x)))


def _mod_kernel(c_ref, w_ref, b_ref, lq1, lk1, lq2, lk2, mod_ref, lam_ref, *, lam_init):
    a = _silu(c_ref[...]).astype(BF16)
    m = jnp.dot(a, w_ref[...].astype(BF16), preferred_element_type=F32)
    mod_ref[...] = m + b_ref[...]
    e1 = jnp.exp(jnp.sum(lq1[...] * lk1[...], axis=-1, keepdims=True))
    e2 = jnp.exp(jnp.sum(lq2[...] * lk2[...], axis=-1, keepdims=True))
    lam_ref[...] = jnp.broadcast_to(e1 - e2 + lam_init, lam_ref.shape)


def _modulation(cvecs, w_ada, b_ada, lq1, lk1, lq2, lk2, lam_init):
    rows, d = cvecs.shape
    n_out = w_ada.shape[1]
    bn = 512
    vec = lambda: pl.BlockSpec((1, HEAD_DIM), lambda j: (0, 0))
    return pl.pallas_call(
        functools.partial(_mod_kernel, lam_init=lam_init),
        grid=(n_out // bn,),
        in_specs=[
            pl.BlockSpec((rows, d), lambda j: (0, 0)),
            pl.BlockSpec((d, bn), lambda j: (0, j)),
            pl.BlockSpec((1, bn), lambda j: (0, j)),
            vec(), vec(), vec(), vec(),
        ],
        out_specs=[
            pl.BlockSpec((rows, bn), lambda j: (0, j)),
            pl.BlockSpec((8, LANE), lambda j: (0, 0)),
        ],
        out_shape=[
            jax.ShapeDtypeStruct((rows, n_out), F32),
            jax.ShapeDtypeStruct((8, LANE), F32),
        ],
        name="modulation",
    )(cvecs, w_ada, b_ada, lq1, lk1, lq2, lk2)


def _norm_rope_t(z_t, w_b, rope):
    outs = []
    f = ROPE_FREQS
    for g in range(z_t.shape[0] // HEAD_DIM):
        zg = z_t[g * HEAD_DIM:(g + 1) * HEAD_DIM, :]
        ms = jnp.mean(zg * zg, axis=0, keepdims=True)
        yg = zg * lax.rsqrt(ms + EPS) * w_b
        if rope is not None:
            cr, sr, cc, sc = rope
            x1r, x2r, x1c, x2c = yg[0:f], yg[f:2 * f], yg[2 * f:3 * f], yg[3 * f:4 * f]
            yg = jnp.concatenate([x1r * cr - x2r * sr, x2r * cr + x1r * sr,
                                  x1c * cc - x2c * sc, x2c * cc + x1c * sc], axis=0)
        outs.append(yg)
    return outs


def _store_vt(vt_ref, v_t):
    t = v_t.shape[1]
    ones_blk = (lax.broadcasted_iota(jnp.int32, (V_EXT - V_DIM, t), 0) == 0).astype(BF16)
    for h in range(v_t.shape[0] // V_DIM):
        vt_ref[0, h * V_EXT:h * V_EXT + V_DIM, :] = v_t[h * V_DIM:(h + 1) * V_DIM].astype(BF16)
        vt_ref[0, h * V_EXT + V_DIM:(h + 1) * V_EXT, :] = ones_blk


def _proj_kernel(*refs, use_rope, emit_cache, append_cache):
    it = iter(refs)
    x_ref, shift_ref, scale_ref, nw_ref, w_ref, qw_ref, kw_ref = (next(it) for _ in range(7))
    rope_refs = tuple(next(it) for _ in range(4)) if use_rope else None
    if append_cache:
        ck_ref, cv_ref = next(it), next(it)
    qt_ref, k_ref, vt_ref, ga_ref, u_ref, gc_ref = (next(it) for _ in range(6))
    if emit_cache:
        kf_ref, vf_ref = next(it), next(it)

    def project():
        x = x_ref[0]
        ms = jnp.mean(x * x, axis=-1, keepdims=True)
        y = x * lax.rsqrt(ms + EPS) * nw_ref[...]
        h = (y * (1.0 + scale_ref[0]) + shift_ref[0]).astype(BF16)
        rope = tuple(r[...] for r in rope_refs) if use_rope else None
        d_att = k_ref.shape[2]

        def zcol(i):
            return jnp.dot(h, w_ref[:, i * d_att:(i + 1) * d_att], preferred_element_type=F32)

        q_groups = _norm_rope_t(zcol(0).T, qw_ref[...], rope)
        zero = jnp.zeros_like(q_groups[0])
        tm = zero.shape[1]
        for g, yg in enumerate(q_groups):
            yg = yg * Q_SCALE
            head, m = divmod(g, 2)
            pair = [yg, zero] if m == 0 else [zero, yg]
            qt_ref[0, head * V_DIM:(head + 1) * V_DIM, m * tm:(m + 1) * tm] = (
                jnp.concatenate(pair, axis=0).astype(BF16))

        k = jnp.concatenate(_norm_rope_t(zcol(1).T, kw_ref[...], rope), axis=0).T
        k_ref[0] = k.astype(BF16)
        v = zcol(2)
        _store_vt(vt_ref, v.T)
        if emit_cache:
            kf_ref[0] = k
            vf_ref[0] = v

        ga_ref[0] = _silu(zcol(3)).astype(BF16)
        cb = zcol(4)
        u_ref[0] = (zcol(5) * zcol(6)).astype(BF16)
        gc_ref[0] = (cb * _silu(zcol(7))).astype(BF16)

    if append_cache:
        last = pl.num_programs(1) - 1
        pl.when(pl.program_id(1) < last)(project)

        @pl.when(pl.program_id(1) == last)
        def _():
            k_ref[0] = ck_ref[0].astype(BF16)
            _store_vt(vt_ref, cv_ref[0].T)
    else:
        project()


def _project(x, mod3, mod_row_fn, norm_w, w_in, qw_b, kw_b, rope_t, emit_cache, cache):
    b, n, d = x.shape
    d_in = w_in.shape[1]
    d_att = d_in // 8
    d_vt = d_att // V_DIM * V_EXT
    tm = TILE
    nt = n // tm
    use_rope = rope_t is not None
    append_cache = cache is not None
    n_keys = n
    tok_t = lambda t: t
    if append_cache:
        assert cache[0].shape[1] == tm
        n_keys = n + tm
        tok_t = lambda t: jnp.minimum(t, nt - 1)

    in_specs = [
        pl.BlockSpec((1, tm, d), lambda i, t: (i, tok_t(t), 0)),
        pl.BlockSpec((1, 1, d), lambda i, t: (mod_row_fn(i), 0, 0)),
        pl.BlockSpec((1, 1, d), lambda i, t: (mod_row_fn(i), 0, 1)),
        pl.BlockSpec((1, d), lambda i, t: (0, 0)),
        pl.BlockSpec((d, d_in), lambda i, t: (0, 0)),
        pl.BlockSpec((HEAD_DIM, tm), lambda i, t: (0, 0)),
        pl.BlockSpec((HEAD_DIM, tm), lambda i, t: (0, 0)),
    ]
    args = [x, mod3, mod3, norm_w, w_in, qw_b, kw_b]
    if use_rope:
        in_specs += [pl.BlockSpec((ROPE_FREQS, tm), lambda i, t: (0, tok_t(t)))] * 4
        args += list(rope_t)
    if append_cache:
        in_specs += [pl.BlockSpec((1, tm, d_att), lambda i, t: (i, 0, 0))] * 2
        args += list(cache)

    tok = lambda: pl.BlockSpec((1, tm, d_att), lambda i, t: (i, tok_t(t), 0))
    out_specs = [
        pl.BlockSpec((1, d_att, 2 * tm), lambda i, t: (i, 0, tok_t(t))),
        pl.BlockSpec((1, tm, d_att), lambda i, t: (i, t, 0)),
        pl.BlockSpec((1, d_vt, tm), lambda i, t: (i, 0, t)),
        tok(), tok(), tok(),
    ]
    out_shape = [
        jax.ShapeDtypeStruct((b, d_att, 2 * n), BF16),
        jax.ShapeDtypeStruct((b, n_keys, d_att), BF16),
        jax.ShapeDtypeStruct((b, d_vt, n_keys), BF16),
        jax.ShapeDtypeStruct((b, n, d_att), BF16),
        jax.ShapeDtypeStruct((b, n, d_att), BF16),
        jax.ShapeDtypeStruct((b, n, d_att), BF16),
    ]
    if emit_cache:
        out_specs += [tok(), tok()]
        out_shape += [jax.ShapeDtypeStruct((b, n, d_att), F32)] * 2

    return pl.pallas_call(
        functools.partial(_proj_kernel, use_rope=use_rope, emit_cache=emit_cache,
                          append_cache=append_cache),
        grid=(b, n_keys // tm),
        in_specs=in_specs,
        out_specs=out_specs,
        out_shape=out_shape,
        compiler_params=pltpu.CompilerParams(
            dimension_semantics=("parallel", "arbitrary"),
            vmem_limit_bytes=VMEM_LIMIT),
        name="project_latent" if use_rope else "project_ctx",
    )(*args)


def _attn_kernel(lam_ref, qt_ref, k_ref, vt_ref, ga_ref, u_ref, up_ref, un_ref, gc_ref, x_ref,
                 gate_ref, sw_ref, cw_ref, cb_ref, wo_ref, out_ref,
                 s_a, s_b, mx_a, mx_b, p_a, p_b, al_a, al_b, st_ref, acc_ref, cat_ref,
                 *, n_new, lam_init):
    tq = x_ref.shape[1]
    n_keys = k_ref.shape[1]
    n_heads = qt_ref.shape[1] // V_DIM
    d_att = n_heads * V_DIM
    lam = lam_ref[0:1, 0:1]

    u = u_ref[0].astype(F32)
    t = pl.program_id(1)
    prev_row = up_ref[0].astype(F32)[BF16_SUBLANES - 1:BF16_SUBLANES, :]
    next_row = un_ref[0].astype(F32)[0:1, :]
    prev_row = jnp.where(t == 0, 0.0, prev_row)
    next_row = jnp.where(t == pl.num_programs(1) - 1, 0.0, next_row)
    rows = lax.broadcasted_iota(jnp.int32, u.shape, 0)
    u_prev = jnp.where(rows == 0, prev_row, pltpu.roll(u, 1, axis=0))
    u_next = jnp.where(rows == tq - 1, next_row, pltpu.roll(u, tq - 1, axis=0))
    conv = u_prev * cw_ref[0:1, :] + u * cw_ref[1:2, :] + u_next * cw_ref[2:3, :] + cb_ref[...]
    conv = (gc_ref[0].astype(F32) * conv).astype(BF16)
    y_conv = jnp.dot(conv, wo_ref[d_att:, :], preferred_element_type=F32)
    out_ref[0] = x_ref[0] + gate_ref[0] * y_conv


    def scores(unit, s_ref, mx_ref):
        off, rows = unit
        for h in range(n_heads):
            hs = slice(h * V_DIM, (h + 1) * V_DIM)
            s = jnp.dot(k_ref[0, pl.ds(off, rows), hs], qt_ref[0, hs, :],
                        preferred_element_type=F32)
            s_ref[h, 0:rows, :] = s
            mx_ref[h] = jnp.max(s.reshape(rows // 8, 8, s.shape[1]), axis=0)

    def softmax_step(unit, s_ref, mx_ref, p_ref, al_ref):
        rows = unit[1]
        for h in range(n_heads):
            m_old = st_ref[h:h + 1, :]
            m_new = jnp.maximum(m_old, jnp.max(mx_ref[h], axis=0, keepdims=True))
            st_ref[h:h + 1, :] = m_new
            al_ref[h:h + 1, :] = jnp.exp2(m_old - m_new)
            p_ref[h, 0:rows, :] = jnp.exp2(s_ref[h, 0:rows, :] - m_new).astype(BF16)

    def accumulate(unit, p_ref, al_ref, first=False):
        off, rows = unit
        for h in range(n_heads):
            pv = jnp.dot(vt_ref[0, h * V_EXT:(h + 1) * V_EXT, pl.ds(off, rows)],
                         p_ref[h, 0:rows, :], preferred_element_type=F32)
            acc_ref[h] = pv if first else al_ref[h:h + 1, :] * acc_ref[h] + pv

    st_ref[...] = jnp.full(st_ref.shape, -jnp.inf, F32)

    buf_a = (s_a, mx_a, p_a, al_a)
    buf_b = (s_b, mx_b, p_b, al_b)

    def stage_s(unit, buf):
        scores(unit, buf[0], buf[1])

    def stage_m(unit, buf):
        softmax_step(unit, *buf)

    def stage_v(unit, buf, first=False):
        accumulate(unit, buf[2], buf[3], first)

    n_full = n_new // KEY_UNIT
    head_units = [(n_new, n_keys - n_new)] if n_keys > n_new else []
    if n_full == 0:
        head_units.append((0, n_new))
    n_units = len(head_units) + n_full

    def unit(i):
        if isinstance(i, int) and i < len(head_units):
            return head_units[i]
        off = (i - len(head_units)) * KEY_UNIT
        return (off if isinstance(i, int) else pl.multiple_of(off, KEY_UNIT)), KEY_UNIT

    if n_units == 1:
        stage_s(unit(0), buf_a)
        stage_m(unit(0), buf_a)
        stage_v(unit(0), buf_a, first=True)
    else:
        assert n_units % 2 == 1 and n_units >= 5 and len(head_units) <= 3
        stage_s(unit(0), buf_a)
        stage_s(unit(1), buf_b)
        stage_m(unit(0), buf_a)
        stage_s(unit(2), buf_a)
        stage_m(unit(1), buf_b)
        stage_v(unit(0), buf_a, first=True)

        def pair(jj, carry):
            c = 2 * jj
            stage_s(unit(c + 1), buf_b)
            stage_m(unit(c), buf_a)
            stage_v(unit(c - 1), buf_b)
            stage_s(unit(c + 2), buf_a)
            stage_m(unit(c + 1), buf_b)
            stage_v(unit(c), buf_a)
            return carry

        lax.fori_loop(1, (n_units - 1) // 2, pair, 0)
        last = n_units - 1
        stage_m(unit(last), buf_a)
        stage_v(unit(last - 1), buf_b)
        stage_v(unit(last), buf_a)

    for h in range(n_heads):
        hs = slice(h * V_DIM, (h + 1) * V_DIM)
        l0 = acc_ref[h, V_DIM:V_DIM + 1, 0:tq]
        l1 = acc_ref[h, V_DIM:V_DIM + 1, tq:2 * tq]
        o_t = (acc_ref[h, 0:V_DIM, 0:tq] * (1.0 / l0)
               - acc_ref[h, 0:V_DIM, tq:2 * tq] * (lam / l1))
        o = o_t.T
        ms = jnp.mean(o * o, axis=-1, keepdims=True)
        att = o * lax.rsqrt(ms + EPS) * sw_ref[...] * (1.0 - lam_init)
        cat_ref[:, hs] = (att * ga_ref[0, :, hs].astype(F32)).astype(BF16)

    y_att = jnp.dot(cat_ref[...], wo_ref[0:d_att, :], preferred_element_type=F32)
    out_ref[0] += gate_ref[0] * y_att


def _attend(lam, qt, k, vt, ga, u, gc, x, mod3, mod_row_fn, subln_w, conv_w, conv_b, w_out,
            lam_init):
    b, n, d = x.shape
    n_keys, d_att = k.shape[1], k.shape[2]
    tq = TILE
    nt = n // tq
    halo = BF16_SUBLANES
    blocks_per_tile = tq // halo
    n_halo_blocks = n // halo
    n_heads = d_att // V_DIM

    tok = lambda: pl.BlockSpec((1, tq, d_att), lambda i, t: (i, t, 0))
    in_specs = [
        pl.BlockSpec((8, LANE), lambda i, t: (0, 0)),
        pl.BlockSpec((1, d_att, 2 * tq), lambda i, t: (i, 0, t)),
        pl.BlockSpec((1, n_keys, d_att), lambda i, t: (i, 0, 0)),
        pl.BlockSpec((1, vt.shape[1], n_keys), lambda i, t: (i, 0, 0)),
        tok(),
        tok(),
        pl.BlockSpec((1, halo, d_att),
                     lambda i, t: (i, jnp.maximum(t * blocks_per_tile - 1, 0), 0)),
        pl.BlockSpec((1, halo, d_att),
                     lambda i, t: (i, jnp.minimum((t + 1) * blocks_per_tile, n_halo_blocks - 1), 0)),
        tok(),
        pl.BlockSpec((1, tq, d), lambda i, t: (i, t, 0)),
        pl.BlockSpec((1, 1, d), lambda i, t: (mod_row_fn(i), 0, 2)),
        pl.BlockSpec((1, V_DIM), lambda i, t: (0, 0)),
        pl.BlockSpec((CONV_W, d_att), lambda i, t: (0, 0)),
        pl.BlockSpec((1, d_att), lambda i, t: (0, 0)),
        pl.BlockSpec((2 * d_att, d), lambda i, t: (0, 0)),
    ]
    args = [lam, qt, k, vt, ga, u, u, u, gc, x, mod3, subln_w, conv_w, conv_b, w_out]
    unit_rows = min(KEY_UNIT, n_keys)
    score_buf = lambda: pltpu.VMEM((n_heads, unit_rows, 2 * tq), F32)
    smax_buf = lambda: pltpu.VMEM((n_heads, 8, 2 * tq), F32)
    prob_buf = lambda: pltpu.VMEM((n_heads, unit_rows, 2 * tq), BF16)
    alpha_buf = lambda: pltpu.VMEM((8, 2 * tq), F32)

    return pl.pallas_call(
        functools.partial(_attn_kernel, n_new=n, lam_init=lam_init),
        grid=(b, nt),
        in_specs=in_specs,
        out_specs=pl.BlockSpec((1, tq, d), lambda i, t: (i, t, 0)),
        out_shape=jax.ShapeDtypeStruct((b, n, d), F32),
        scratch_shapes=[
            score_buf(), score_buf(), smax_buf(), smax_buf(), prob_buf(), prob_buf(),
            alpha_buf(), alpha_buf(),
            alpha_buf(),
            pltpu.VMEM((n_heads, V_EXT, 2 * tq), F32),
            pltpu.VMEM((tq, d_att), BF16),
        ],
        compiler_params=pltpu.CompilerParams(
            dimension_semantics=("parallel", "arbitrary"),
            vmem_limit_bytes=VMEM_LIMIT),
        name="attend_latent" if n_keys > n else "attend_ctx",
    )(*args)


def _rope_tables_t(n):
    t = jnp.arange(n)
    row = (t // GRID_W).astype(F32)
    col = (t % GRID_W).astype(F32)
    inv = 1.0 / (ROPE_BASE ** (jnp.arange(ROPE_FREQS, dtype=F32) / ROPE_FREQS))
    ang_r = row[:, None] * inv
    ang_c = col[:, None] * inv
    return (jnp.cos(ang_r).T, jnp.sin(ang_r).T, jnp.cos(ang_c).T, jnp.sin(ang_c).T)


def kernel(x_prompt, x_sample, cache_k, cache_v, c, c_ctx, norm_w, w_ada, b_ada, w_in,
           q_norm_w, k_norm_w, lambda_q1, lambda_k1, lambda_q2, lambda_k2, subln_w,
           conv_w, conv_b, w_out):
    depth = norm_w.shape[0]
    assert depth == 1
    i = 0
    lam_init = 0.8 - 0.6 * math.exp(-0.3 * i)
    b_ctx, n_ctx, d = x_prompt.shape
    b_lat, n_lat, _ = x_sample.shape
    d_att = w_in.shape[2] // 8
    n_heads = d_att // V_DIM

    n_rows = 16
    cvecs = jnp.zeros((n_rows, d), F32).at[:b_lat].set(c).at[b_lat].set(c_ctx)
    mod, lam = _modulation(cvecs, w_ada[i], b_ada[i][None], lambda_q1[i][None], lambda_k1[i][None],
                           lambda_q2[i][None], lambda_k2[i][None], lam_init)
    mod3 = mod.reshape(n_rows, 1, 3 * d)

    w_in_b = w_in[i].astype(BF16)
    w_out_b = w_out[i].astype(BF16)
    nw = norm_w[i][None]
    qw_b = jnp.broadcast_to(q_norm_w[i][:, None], (HEAD_DIM, TILE))
    kw_b = jnp.broadcast_to(k_norm_w[i][:, None], (HEAD_DIM, TILE))
    sw = subln_w[i][None]
    cw = conv_w[i]
    cb = conv_b[i][None]

    ctx_row = lambda bi: b_lat
    lat_row = lambda bi: bi

    qt, k, vt, ga, u, gc, kf, vf = _project(x_prompt, mod3, ctx_row, nw, w_in_b, qw_b, kw_b,
                                            None, True, None)
    y_prompt = _attend(lam, qt, k, vt, ga, u, gc, x_prompt, mod3, ctx_row, sw, cw, cb, w_out_b,
                       lam_init)
    new_cache_k = kf.reshape(b_ctx, 1, n_ctx, n_heads, 2, HEAD_DIM)
    new_cache_v = vf.reshape(b_ctx, 1, n_ctx, n_heads, V_DIM)

    past = cache_k.shape[2]
    cache = (cache_k[:, i].reshape(b_lat, past, d_att), cache_v[:, i].reshape(b_lat, past, d_att))
    qt, k, vt, ga, u, gc = _project(x_sample, mod3, lat_row, nw, w_in_b, qw_b, kw_b,
                                    _rope_tables_t(n_lat), False, cache)
    y_sample = _attend(lam, qt, k, vt, ga, u, gc, x_sample, mod3, lat_row, sw, cw, cb, w_out_b,
                       lam_init)
    return (y_prompt, y_sample, new_cache_k, new_cache_v)
```

```python
import functools
import math

import jax
import jax.numpy as jnp
from jax import lax
from jax.experimental import pallas as pl
from jax.experimental.pallas import tpu as pltpu

F32 = jnp.float32
BF16 = jnp.bfloat16

HEAD_DIM = 64
V_DIM = 2 * HEAD_DIM
GRID_W = 64
ROPE_FREQS = HEAD_DIM // 4
ROPE_BASE = 10000.0
EPS = 1e-6
CONV_W = 3
LANE = 128
BF16_SUBLANES = 16
VMEM_LIMIT = 56 * 1024 * 1024
TILE = 512
PROJ_TILE = 512
KEY_UNIT = 256
Q_SCALE = HEAD_DIM ** -0.5 * math.log2(math.e)
V_EXT = V_DIM + BF16_SUBLANES


def _silu(x):
    return x * (1.0 / (1.0 + jnp.exp(-x)))


def _mod_kernel(c_ref, w_ref, b_ref, lq1, lk1, lq2, lk2, mod_ref, lam_ref, *, lam_init):
    a = _silu(c_ref[...]).astype(BF16)
    m = jnp.dot(a, w_ref[...].astype(BF16), preferred_element_type=F32)
    mod_ref[...] = m + b_ref[...]
    e1 = jnp.exp(jnp.sum(lq1[...] * lk1[...], axis=-1, keepdims=True))
    e2 = jnp.exp(jnp.sum(lq2[...] * lk2[...], axis=-1, keepdims=True))
    lam_ref[...] = jnp.broadcast_to(e1 - e2 + lam_init, lam_ref.shape)


def _modulation(cvecs, w_ada, b_ada, lq1, lk1, lq2, lk2, lam_init):
    rows, d = cvecs.shape
    n_out = w_ada.shape[1]
    bn = 512
    vec = lambda: pl.BlockSpec((1, HEAD_DIM), lambda j: (0, 0))
    return pl.pallas_call(
        functools.partial(_mod_kernel, lam_init=lam_init),
        grid=(n_out // bn,),
        in_specs=[
            pl.BlockSpec((rows, d), lambda j: (0, 0)),
            pl.BlockSpec((d, bn), lambda j: (0, j)),
            pl.BlockSpec((1, bn), lambda j: (0, j)),
            vec(), vec(), vec(), vec(),
        ],
        out_specs=[
            pl.BlockSpec((rows, bn), lambda j: (0, j)),
            pl.BlockSpec((8, LANE), lambda j: (0, 0)),
        ],
        out_shape=[
            jax.ShapeDtypeStruct((rows, n_out), F32),
            jax.ShapeDtypeStruct((8, LANE), F32),
        ],
        name="modulation",
    )(cvecs, w_ada, b_ada, lq1, lk1, lq2, lk2)


def _norm_rope_t(z_t, w_b, rope):
    outs = []
    f = ROPE_FREQS
    for g in range(z_t.shape[0] // HEAD_DIM):
        zg = z_t[g * HEAD_DIM:(g + 1) * HEAD_DIM, :]
        ms = jnp.mean(zg * zg, axis=0, keepdims=True)
        yg = zg * lax.rsqrt(ms + EPS) * w_b
        if rope is not None:
            cr, sr, cc, sc = rope
            x1r, x2r, x1c, x2c = yg[0:f], yg[f:2 * f], yg[2 * f:3 * f], yg[3 * f:4 * f]
            yg = jnp.concatenate([x1r * cr - x2r * sr, x2r * cr + x1r * sr,
                                  x1c * cc - x2c * sc, x2c * cc + x1c * sc], axis=0)
        outs.append(yg)
    return outs


def _store_vt(vt_ref, v_t):
    t = v_t.shape[1]
    ones_blk = (lax.broadcasted_iota(jnp.int32, (V_EXT - V_DIM, t), 0) == 0).astype(BF16)
    for h in range(v_t.shape[0] // V_DIM):
        vt_ref[0, h * V_EXT:h * V_EXT + V_DIM, 0:t] = v_t[h * V_DIM:(h + 1) * V_DIM].astype(BF16)
        vt_ref[0, h * V_EXT + V_DIM:(h + 1) * V_EXT, 0:t] = ones_blk


def _proj_kernel(*refs, use_rope, emit_cache, append_cache, q_tile):
    it = iter(refs)
    x_ref, shift_ref, scale_ref, nw_ref, w_ref, qw_ref, kw_ref = (next(it) for _ in range(7))
    rope_refs = tuple(next(it) for _ in range(4)) if use_rope else None
    if append_cache:
        ck_ref, cv_ref = next(it), next(it)
    qt_ref, k_ref, vt_ref, ga_ref, u_ref, gc_ref = (next(it) for _ in range(6))
    if emit_cache:
        kf_ref, vf_ref = next(it), next(it)

    def project():
        x = x_ref[0]
        ms = jnp.mean(x * x, axis=-1, keepdims=True)
        y = x * lax.rsqrt(ms + EPS) * nw_ref[...]
        h = (y * (1.0 + scale_ref[0]) + shift_ref[0]).astype(BF16)
        rope = tuple(r[...] for r in rope_refs) if use_rope else None
        d_att = k_ref.shape[2]

        def zcol(i):
            return jnp.dot(h, w_ref[:, i * d_att:(i + 1) * d_att], preferred_element_type=F32)

        q_groups = _norm_rope_t(zcol(0).T, qw_ref[...], rope)
        zero = jnp.zeros_like(q_groups[0])
        for g, yg in enumerate(q_groups):
            head, m = divmod(g, 2)
            yg = yg * Q_SCALE
            padded = jnp.concatenate([yg, zero] if m == 0 else [zero, yg], axis=0).astype(BF16)
            for j in range(padded.shape[1] // q_tile):
                col = (2 * j + m) * q_tile
                qt_ref[0, head * V_DIM:(head + 1) * V_DIM, col:col + q_tile] = (
                    padded[:, j * q_tile:(j + 1) * q_tile])

        k = jnp.concatenate(_norm_rope_t(zcol(1).T, kw_ref[...], rope), axis=0).T
        k_ref[0] = k.astype(BF16)
        v = zcol(2)
        _store_vt(vt_ref, v.T)
        if emit_cache:
            kf_ref[0] = k
            vf_ref[0] = v

        ga_ref[0] = _silu(zcol(3)).astype(BF16)
        cb = zcol(4)
        u_ref[0] = (zcol(5) * zcol(6)).astype(BF16)
        gc_ref[0] = (cb * _silu(zcol(7))).astype(BF16)

    if append_cache:
        last = pl.num_programs(1) - 1
        pl.when(pl.program_id(1) < last)(project)

        @pl.when(pl.program_id(1) == last)
        def _():
            past = ck_ref.shape[1]
            k_ref[0] = jnp.zeros(k_ref.shape[1:], BF16)
            vt_ref[0] = jnp.zeros(vt_ref.shape[1:], BF16)
            k_ref[0, 0:past, :] = ck_ref[0].astype(BF16)
            _store_vt(vt_ref, cv_ref[0].T)
    else:
        project()


def _project(x, mod3, mod_row_fn, norm_w, w_in, q_norm_w, k_norm_w, rope_t, emit_cache, cache,
             q_tile):
    b, n, d = x.shape
    d_in = w_in.shape[1]
    d_att = d_in // 8
    d_vt = d_att // V_DIM * V_EXT
    tm = min(PROJ_TILE, n)
    nt = n // tm
    use_rope = rope_t is not None
    append_cache = cache is not None
    n_keys = n
    tok_t = lambda t: t
    if append_cache:
        assert cache[0].shape[1] <= tm
        n_keys = n + tm
        tok_t = lambda t: jnp.minimum(t, nt - 1)
    qw_b = jnp.broadcast_to(q_norm_w[:, None], (HEAD_DIM, tm))
    kw_b = jnp.broadcast_to(k_norm_w[:, None], (HEAD_DIM, tm))

    in_specs = [
        pl.BlockSpec((1, tm, d), lambda i, t: (i, tok_t(t), 0)),
        pl.BlockSpec((1, 1, d), lambda i, t: (mod_row_fn(i), 0, 0)),
        pl.BlockSpec((1, 1, d), lambda i, t: (mod_row_fn(i), 0, 1)),
        pl.BlockSpec((1, d), lambda i, t: (0, 0)),
        pl.BlockSpec((d, d_in), lambda i, t: (0, 0)),
        pl.BlockSpec((HEAD_DIM, tm), lambda i, t: (0, 0)),
        pl.BlockSpec((HEAD_DIM, tm), lambda i, t: (0, 0)),
    ]
    args = [x, mod3, mod3, norm_w, w_in, qw_b, kw_b]
    if use_rope:
        in_specs += [pl.BlockSpec((ROPE_FREQS, tm), lambda i, t: (0, tok_t(t)))] * 4
        args += list(rope_t)
    if append_cache:
        in_specs += [pl.BlockSpec((1, cache[0].shape[1], d_att), lambda i, t: (i, 0, 0))] * 2
        args += list(cache)

    tok = lambda: pl.BlockSpec((1, tm, d_att), lambda i, t: (i, tok_t(t), 0))
    out_specs = [
        pl.BlockSpec((1, d_att, 2 * tm), lambda i, t: (i, 0, tok_t(t))),
        pl.BlockSpec((1, tm, d_att), lambda i, t: (i, t, 0)),
        pl.BlockSpec((1, d_vt, tm), lambda i, t: (i, 0, t)),
        tok(), tok(), tok(),
    ]
    out_shape = [
        jax.ShapeDtypeStruct((b, d_att, 2 * n), BF16),
        jax.ShapeDtypeStruct((b, n_keys, d_att), BF16),
        jax.ShapeDtypeStruct((b, d_vt, n_keys), BF16),
        jax.ShapeDtypeStruct((b, n, d_att), BF16),
        jax.ShapeDtypeStruct((b, n, d_att), BF16),
        jax.ShapeDtypeStruct((b, n, d_att), BF16),
    ]
    if emit_cache:
        out_specs += [tok(), tok()]
        out_shape += [jax.ShapeDtypeStruct((b, n, d_att), F32)] * 2

    return pl.pallas_call(
        functools.partial(_proj_kernel, use_rope=use_rope, emit_cache=emit_cache,
                          append_cache=append_cache, q_tile=q_tile),
        grid=(b, n_keys // tm),
        in_specs=in_specs,
        out_specs=out_specs,
        out_shape=out_shape,
        compiler_params=pltpu.CompilerParams(
            dimension_semantics=("parallel", "arbitrary"),
            vmem_limit_bytes=VMEM_LIMIT),
        name="project_latent" if use_rope else "project_ctx",
    )(*args)


def _attn_kernel(lam_ref, qt_ref, k_ref, vt_ref, ga_ref, u_ref, up_ref, un_ref, gc_ref, x_ref,
                 gate_ref, sw_ref, cw_ref, cb_ref, wo_ref, out_ref,
                 s_a, s_b, mx_a, mx_b, p_a, p_b, al_a, al_b, st_ref, acc_ref, cat_ref,
                 *, n_new, n_keys, lam_init):
    tq = x_ref.shape[1]
    n_heads = qt_ref.shape[1] // V_DIM
    d_att = n_heads * V_DIM
    lam = lam_ref[0:1, 0:1]

    def conv_branch():
        u = u_ref[0].astype(F32)
        t = pl.program_id(1)
        prev_row = up_ref[0].astype(F32)[BF16_SUBLANES - 1:BF16_SUBLANES, :]
        next_row = un_ref[0].astype(F32)[0:1, :]
        prev_row = jnp.where(t == 0, 0.0, prev_row)
        next_row = jnp.where(t == pl.num_programs(1) - 1, 0.0, next_row)
        rows = lax.broadcasted_iota(jnp.int32, u.shape, 0)
        u_prev = jnp.where(rows == 0, prev_row, pltpu.roll(u, 1, axis=0))
        u_next = jnp.where(rows == tq - 1, next_row, pltpu.roll(u, tq - 1, axis=0))
        conv = (u_prev * cw_ref[0:1, :] + u * cw_ref[1:2, :] + u_next * cw_ref[2:3, :]
                + cb_ref[...])
        conv = (gc_ref[0].astype(F32) * conv).astype(BF16)
        y_conv = jnp.dot(conv, wo_ref[d_att:, :], preferred_element_type=F32)
        out_ref[0] = x_ref[0] + gate_ref[0] * y_conv


    def scores(unit, s_ref, mx_ref):
        off, rows = unit
        for h in range(n_heads):
            hs = slice(h * V_DIM, (h + 1) * V_DIM)
            s = jnp.dot(k_ref[0, pl.ds(off, rows), hs], qt_ref[0, hs, :],
                        preferred_element_type=F32)
            s_ref[h, 0:rows, :] = s
            mx_ref[h] = jnp.max(s.reshape(rows // 8, 8, s.shape[1]), axis=0)

    def softmax_step(unit, s_ref, mx_ref, p_ref, al_ref):
        rows = unit[1]
        for h in range(n_heads):
            m_old = st_ref[h:h + 1, :]
            m_new = jnp.maximum(m_old, jnp.max(mx_ref[h], axis=0, keepdims=True))
            st_ref[h:h + 1, :] = m_new
            al_ref[h:h + 1, :] = jnp.exp2(m_old - m_new)
            p_ref[h, 0:rows, :] = jnp.exp2(s_ref[h, 0:rows, :] - m_new).astype(BF16)

    def accumulate(unit, p_ref, al_ref, first=False):
        off, rows = unit
        for h in range(n_heads):
            pv = jnp.dot(vt_ref[0, h * V_EXT:(h + 1) * V_EXT, pl.ds(off, rows)],
                         p_ref[h, 0:rows, :], preferred_element_type=F32)
            acc_ref[h] = pv if first else al_ref[h:h + 1, :] * acc_ref[h] + pv

    st_ref[...] = jnp.full(st_ref.shape, -jnp.inf, F32)

    buf_a = (s_a, mx_a, p_a, al_a)
    buf_b = (s_b, mx_b, p_b, al_b)

    def stage_s(unit, buf):
        scores(unit, buf[0], buf[1])

    def stage_m(unit, buf):
        softmax_step(unit, *buf)

    def stage_v(unit, buf, first=False):
        accumulate(unit, buf[2], buf[3], first)

    n_full = n_new // KEY_UNIT
    head_units = [(n_new, n_keys - n_new)] if n_keys > n_new else []
    if n_full == 0:
        head_units.append((0, n_new))
    n_units = len(head_units) + n_full

    def unit(i):
        if isinstance(i, int) and i < len(head_units):
            return head_units[i]
        off = (i - len(head_units)) * KEY_UNIT
        return (off if isinstance(i, int) else pl.multiple_of(off, KEY_UNIT)), KEY_UNIT

    if n_units == 1:
        conv_branch()
        stage_s(unit(0), buf_a)
        stage_m(unit(0), buf_a)
        stage_v(unit(0), buf_a, first=True)
    else:
        assert n_units % 2 == 1 and n_units >= 5 and len(head_units) <= 3
        stage_s(unit(0), buf_a)
        stage_s(unit(1), buf_b)
        stage_m(unit(0), buf_a)
        stage_s(unit(2), buf_a)
        stage_m(unit(1), buf_b)
        stage_v(unit(0), buf_a, first=True)

        def pair(jj, carry):
            c = 2 * jj
            stage_s(unit(c + 1), buf_b)
            stage_m(unit(c), buf_a)
            stage_v(unit(c - 1), buf_b)
            stage_s(unit(c + 2), buf_a)
            stage_m(unit(c + 1), buf_b)
            stage_v(unit(c), buf_a)
            return carry

        lax.fori_loop(1, (n_units - 1) // 2, pair, 0)
        last = n_units - 1
        stage_m(unit(last), buf_a)
        stage_v(unit(last - 1), buf_b)
        stage_v(unit(last), buf_a)

    for h in range(n_heads):
        hs = slice(h * V_DIM, (h + 1) * V_DIM)
        l0 = acc_ref[h, V_DIM:V_DIM + 1, 0:tq]
        l1 = acc_ref[h, V_DIM:V_DIM + 1, tq:2 * tq]
        o_t = (acc_ref[h, 0:V_DIM, 0:tq] * (1.0 / l0)
               - acc_ref[h, 0:V_DIM, tq:2 * tq] * (lam / l1))
        o = o_t.T
        ms = jnp.mean(o * o, axis=-1, keepdims=True)
        att = o * lax.rsqrt(ms + EPS) * sw_ref[...] * (1.0 - lam_init)
        cat_ref[:, hs] = (att * ga_ref[0, :, hs].astype(F32)).astype(BF16)

    if n_units > 1:
        conv_branch()
    y_att = jnp.dot(cat_ref[...], wo_ref[0:d_att, :], preferred_element_type=F32)
    out_ref[0] += gate_ref[0] * y_att


def _attend(lam, qt, k, vt, n_keys, ga, u, gc, x, mod3, mod_row_fn, subln_w, conv_w, conv_b,
            w_out, lam_init):
    b, n, d = x.shape
    d_att = k.shape[2]
    tq = min(TILE, n)
    nt = n // tq
    halo = BF16_SUBLANES
    blocks_per_tile = tq // halo
    n_halo_blocks = n // halo
    n_heads = d_att // V_DIM

    tok = lambda: pl.BlockSpec((1, tq, d_att), lambda i, t: (i, t, 0))
    in_specs = [
        pl.BlockSpec((8, LANE), lambda i, t: (0, 0)),
        pl.BlockSpec((1, d_att, 2 * tq), lambda i, t: (i, 0, t)),
        pl.BlockSpec((1, k.shape[1], d_att), lambda i, t: (i, 0, 0)),
        pl.BlockSpec((1, vt.shape[1], vt.shape[2]), lambda i, t: (i, 0, 0)),
        tok(),
        tok(),
        pl.BlockSpec((1, halo, d_att),
                     lambda i, t: (i, jnp.maximum(t * blocks_per_tile - 1, 0), 0)),
        pl.BlockSpec((1, halo, d_att),
                     lambda i, t: (i, jnp.minimum((t + 1) * blocks_per_tile, n_halo_blocks - 1), 0)),
        tok(),
        pl.BlockSpec((1, tq, d), lambda i, t: (i, t, 0)),
        pl.BlockSpec((1, 1, d), lambda i, t: (mod_row_fn(i), 0, 2)),
        pl.BlockSpec((1, V_DIM), lambda i, t: (0, 0)),
        pl.BlockSpec((CONV_W, d_att), lambda i, t: (0, 0)),
        pl.BlockSpec((1, d_att), lambda i, t: (0, 0)),
        pl.BlockSpec((2 * d_att, d), lambda i, t: (0, 0)),
    ]
    args = [lam, qt, k, vt, ga, u, u, u, gc, x, mod3, subln_w, conv_w, conv_b, w_out]
    unit_rows = min(KEY_UNIT, n_keys)
    score_buf = lambda: pltpu.VMEM((n_heads, unit_rows, 2 * tq), F32)
    smax_buf = lambda: pltpu.VMEM((n_heads, 8, 2 * tq), F32)
    prob_buf = lambda: pltpu.VMEM((n_heads, unit_rows, 2 * tq), BF16)
    alpha_buf = lambda: pltpu.VMEM((8, 2 * tq), F32)

    return pl.pallas_call(
        functools.partial(_attn_kernel, n_new=n, n_keys=n_keys, lam_init=lam_init),
        grid=(b, nt),
        in_specs=in_specs,
        out_specs=pl.BlockSpec((1, tq, d), lambda i, t: (i, t, 0)),
        out_shape=jax.ShapeDtypeStruct((b, n, d), F32),
        scratch_shapes=[
            score_buf(), score_buf(), smax_buf(), smax_buf(), prob_buf(), prob_buf(),
            alpha_buf(), alpha_buf(),
            alpha_buf(),
            pltpu.VMEM((n_heads, V_EXT, 2 * tq), F32),
            pltpu.VMEM((tq, d_att), BF16),
        ],
        compiler_params=pltpu.CompilerParams(
            dimension_semantics=("parallel", "arbitrary"),
            vmem_limit_bytes=VMEM_LIMIT),
        name="attend_latent" if n_keys > n else "attend_ctx",
    )(*args)


def _rope_tables_t(n):
    t = jnp.arange(n)
    row = (t // GRID_W).astype(F32)
    col = (t % GRID_W).astype(F32)
    inv = 1.0 / (ROPE_BASE ** (jnp.arange(ROPE_FREQS, dtype=F32) / ROPE_FREQS))
    ang_r = row[:, None] * inv
    ang_c = col[:, None] * inv
    return (jnp.cos(ang_r).T, jnp.sin(ang_r).T, jnp.cos(ang_c).T, jnp.sin(ang_c).T)


def kernel(x_prompt, x_sample, cache_k, cache_v, c, c_ctx, norm_w, w_ada, b_ada, w_in,
           q_norm_w, k_norm_w, lambda_q1, lambda_k1, lambda_q2, lambda_k2, subln_w,
           conv_w, conv_b, w_out):
    depth = norm_w.shape[0]
    assert depth == 1
    i = 0
    lam_init = 0.8 - 0.6 * math.exp(-0.3 * i)
    b_ctx, n_ctx, d = x_prompt.shape
    b_lat, n_lat, _ = x_sample.shape
    d_att = w_in.shape[2] // 8
    n_heads = d_att // V_DIM

    n_rows = 16
    cvecs = jnp.zeros((n_rows, d), F32).at[:b_lat].set(c).at[b_lat].set(c_ctx)
    mod, lam = _modulation(cvecs, w_ada[i], b_ada[i][None], lambda_q1[i][None], lambda_k1[i][None],
                           lambda_q2[i][None], lambda_k2[i][None], lam_init)
    mod3 = mod.reshape(n_rows, 1, 3 * d)

    w_in_b = w_in[i].astype(BF16)
    w_out_b = w_out[i].astype(BF16)
    nw = norm_w[i][None]
    qw, kw = q_norm_w[i], k_norm_w[i]
    sw = subln_w[i][None]
    cw = conv_w[i]
    cb = conv_b[i][None]

    ctx_row = lambda bi: b_lat
    lat_row = lambda bi: bi

    qt, k, vt, ga, u, gc, kf, vf = _project(x_prompt, mod3, ctx_row, nw, w_in_b, qw, kw,
                                            None, True, None, min(TILE, n_ctx))
    y_prompt = _attend(lam, qt, k, vt, n_ctx, ga, u, gc, x_prompt, mod3, ctx_row, sw, cw, cb,
                       w_out_b, lam_init)
    new_cache_k = kf.reshape(b_ctx, 1, n_ctx, n_heads, 2, HEAD_DIM)
    new_cache_v = vf.reshape(b_ctx, 1, n_ctx, n_heads, V_DIM)

    past = cache_k.shape[2]
    cache = (cache_k[:, i].reshape(b_lat, past, d_att), cache_v[:, i].reshape(b_lat, past, d_att))
    qt, k, vt, ga, u, gc = _project(x_sample, mod3, lat_row, nw, w_in_b, qw, kw,
                                    _rope_tables_t(n_lat), False, cache, min(TILE, n_lat))
    y_sample = _attend(lam, qt, k, vt, n_lat + past, ga, u, gc, x_sample, mod3, lat_row, sw, cw, cb,
                       w_out_b, lam_init)
    return (y_prompt, y_sample, new_cache_k, new_cache_v)
```

```python
import functools
import math

import jax
import jax.numpy as jnp
from jax import lax
from jax.experimental import pallas as pl
from jax.experimental.pallas import tpu as pltpu

F32 = jnp.float32
BF16 = jnp.bfloat16

HEAD_DIM = 64
V_DIM = 2 * HEAD_DIM
GRID_W = 64
ROPE_FREQS = HEAD_DIM // 4
ROPE_BASE = 10000.0
EPS = 1e-6
CONV_W = 3
LANE = 128
BF16_SUBLANES = 16
VMEM_LIMIT = 56 * 1024 * 1024
TILE = 512
PROJ_TILE = 512
KEY_UNIT = 256
FAST_RANGE = 64.0
Q_SCALE = HEAD_DIM ** -0.5 * math.log2(math.e)
V_EXT = V_DIM + BF16_SUBLANES


def _silu(x):
    return x * (1.0 / (1.0 + jnp.exp(-x)))


def _mod_kernel(c_ref, w_ref, b_ref, lq1, lk1, lq2, lk2, mod_ref, lam_ref, *, lam_init):
    a = _silu(c_ref[...]).astype(BF16)
    m = jnp.dot(a, w_ref[...].astype(BF16), preferred_element_type=F32)
    mod_ref[...] = m + b_ref[...]
    e1 = jnp.exp(jnp.sum(lq1[...] * lk1[...], axis=-1, keepdims=True))
    e2 = jnp.exp(jnp.sum(lq2[...] * lk2[...], axis=-1, keepdims=True))
    lam_ref[...] = jnp.broadcast_to(e1 - e2 + lam_init, lam_ref.shape)


def _modulation(cvecs, w_ada, b_ada, lq1, lk1, lq2, lk2, lam_init):
    rows, d = cvecs.shape
    n_out = w_ada.shape[1]
    bn = 512
    vec = lambda: pl.BlockSpec((1, HEAD_DIM), lambda j: (0, 0))
    return pl.pallas_call(
        functools.partial(_mod_kernel, lam_init=lam_init),
        grid=(n_out // bn,),
        in_specs=[
            pl.BlockSpec((rows, d), lambda j: (0, 0)),
            pl.BlockSpec((d, bn), lambda j: (0, j)),
            pl.BlockSpec((1, bn), lambda j: (0, j)),
            vec(), vec(), vec(), vec(),
        ],
        out_specs=[
            pl.BlockSpec((rows, bn), lambda j: (0, j)),
            pl.BlockSpec((8, LANE), lambda j: (0, 0)),
        ],
        out_shape=[
            jax.ShapeDtypeStruct((rows, n_out), F32),
            jax.ShapeDtypeStruct((8, LANE), F32),
        ],
        name="modulation",
    )(cvecs, w_ada, b_ada, lq1, lk1, lq2, lk2)


def _norm_rope_t(z_t, w_b, rope):
    outs = []
    f = ROPE_FREQS
    for g in range(z_t.shape[0] // HEAD_DIM):
        zg = z_t[g * HEAD_DIM:(g + 1) * HEAD_DIM, :]
        ms = jnp.mean(zg * zg, axis=0, keepdims=True)
        yg = zg * lax.rsqrt(ms + EPS) * w_b
        if rope is not None:
            cr, sr, cc, sc = rope
            x1r, x2r, x1c, x2c = yg[0:f], yg[f:2 * f], yg[2 * f:3 * f], yg[3 * f:4 * f]
            yg = jnp.concatenate([x1r * cr - x2r * sr, x2r * cr + x1r * sr,
                                  x1c * cc - x2c * sc, x2c * cc + x1c * sc], axis=0)
        outs.append(yg)
    return outs


def _store_vt(vt_ref, v_t):
    t = v_t.shape[1]
    ones_blk = (lax.broadcasted_iota(jnp.int32, (V_EXT - V_DIM, t), 0) == 0).astype(BF16)
    for h in range(v_t.shape[0] // V_DIM):
        vt_ref[0, h * V_EXT:h * V_EXT + V_DIM, 0:t] = v_t[h * V_DIM:(h + 1) * V_DIM].astype(BF16)
        vt_ref[0, h * V_EXT + V_DIM:(h + 1) * V_EXT, 0:t] = ones_blk


def _proj_kernel(*refs, use_rope, emit_cache, append_cache, q_tile):
    it = iter(refs)
    x_ref, shift_ref, scale_ref, nw_ref, w_ref, qw_ref, kw_ref = (next(it) for _ in range(7))
    rope_refs = tuple(next(it) for _ in range(4)) if use_rope else None
    if append_cache:
        ck_ref, cv_ref = next(it), next(it)
    qt_ref, k_ref, vt_ref, ga_ref, u_ref, gc_ref = (next(it) for _ in range(6))
    if emit_cache:
        kf_ref, vf_ref = next(it), next(it)

    def project():
        x = x_ref[0]
        ms = jnp.mean(x * x, axis=-1, keepdims=True)
        y = x * lax.rsqrt(ms + EPS) * nw_ref[...]
        h = (y * (1.0 + scale_ref[0]) + shift_ref[0]).astype(BF16)
        rope = tuple(r[...] for r in rope_refs) if use_rope else None
        d_att = k_ref.shape[2]

        def zcol(i):
            return jnp.dot(h, w_ref[:, i * d_att:(i + 1) * d_att], preferred_element_type=F32)

        q_groups = _norm_rope_t(zcol(0).T, qw_ref[...], rope)
        zero = jnp.zeros_like(q_groups[0])
        for g, yg in enumerate(q_groups):
            head, m = divmod(g, 2)
            yg = yg * Q_SCALE
            padded = jnp.concatenate([yg, zero] if m == 0 else [zero, yg], axis=0).astype(BF16)
            for j in range(padded.shape[1] // q_tile):
                col = (2 * j + m) * q_tile
                qt_ref[0, head * V_DIM:(head + 1) * V_DIM, col:col + q_tile] = (
                    padded[:, j * q_tile:(j + 1) * q_tile])

        k = jnp.concatenate(_norm_rope_t(zcol(1).T, kw_ref[...], rope), axis=0).T
        k_ref[0] = k.astype(BF16)
        v = zcol(2)
        _store_vt(vt_ref, v.T)
        if emit_cache:
            kf_ref[0] = k
            vf_ref[0] = v

        ga_ref[0] = _silu(zcol(3)).astype(BF16)
        cb = zcol(4)
        u_ref[0] = (zcol(5) * zcol(6)).astype(BF16)
        gc_ref[0] = (cb * _silu(zcol(7))).astype(BF16)

    if append_cache:
        last = pl.num_programs(1) - 1
        pl.when(pl.program_id(1) < last)(project)

        @pl.when(pl.program_id(1) == last)
        def _():
            past = ck_ref.shape[1]
            k_ref[0] = jnp.zeros(k_ref.shape[1:], BF16)
            vt_ref[0] = jnp.zeros(vt_ref.shape[1:], BF16)
            k_ref[0, 0:past, :] = ck_ref[0].astype(BF16)
            _store_vt(vt_ref, cv_ref[0].T)
    else:
        project()


def _project(x, mod3, mod_row_fn, norm_w, w_in, q_norm_w, k_norm_w, rope_t, emit_cache, cache,
             q_tile):
    b, n, d = x.shape
    d_in = w_in.shape[1]
    d_att = d_in // 8
    d_vt = d_att // V_DIM * V_EXT
    tm = min(PROJ_TILE, n)
    nt = n // tm
    use_rope = rope_t is not None
    append_cache = cache is not None
    n_keys = n
    tok_t = lambda t: t
    if append_cache:
        assert cache[0].shape[1] <= tm
        n_keys = n + tm
        tok_t = lambda t: jnp.minimum(t, nt - 1)
    qw_b = jnp.broadcast_to(q_norm_w[:, None], (HEAD_DIM, tm))
    kw_b = jnp.broadcast_to(k_norm_w[:, None], (HEAD_DIM, tm))

    in_specs = [
        pl.BlockSpec((1, tm, d), lambda i, t: (i, tok_t(t), 0)),
        pl.BlockSpec((1, 1, d), lambda i, t: (mod_row_fn(i), 0, 0)),
        pl.BlockSpec((1, 1, d), lambda i, t: (mod_row_fn(i), 0, 1)),
        pl.BlockSpec((1, d), lambda i, t: (0, 0)),
        pl.BlockSpec((d, d_in), lambda i, t: (0, 0)),
        pl.BlockSpec((HEAD_DIM, tm), lambda i, t: (0, 0)),
        pl.BlockSpec((HEAD_DIM, tm), lambda i, t: (0, 0)),
    ]
    args = [x, mod3, mod3, norm_w, w_in, qw_b, kw_b]
    if use_rope:
        in_specs += [pl.BlockSpec((ROPE_FREQS, tm), lambda i, t: (0, tok_t(t)))] * 4
        args += list(rope_t)
    if append_cache:
        in_specs += [pl.BlockSpec((1, cache[0].shape[1], d_att), lambda i, t: (i, 0, 0))] * 2
        args += list(cache)

    tok = lambda: pl.BlockSpec((1, tm, d_att), lambda i, t: (i, tok_t(t), 0))
    out_specs = [
        pl.BlockSpec((1, d_att, 2 * tm), lambda i, t: (i, 0, tok_t(t))),
        pl.BlockSpec((1, tm, d_att), lambda i, t: (i, t, 0)),
        pl.BlockSpec((1, d_vt, tm), lambda i, t: (i, 0, t)),
        tok(), tok(), tok(),
    ]
    out_shape = [
        jax.ShapeDtypeStruct((b, d_att, 2 * n), BF16),
        jax.ShapeDtypeStruct((b, n_keys, d_att), BF16),
        jax.ShapeDtypeStruct((b, d_vt, n_keys), BF16),
        jax.ShapeDtypeStruct((b, n, d_att), BF16),
        jax.ShapeDtypeStruct((b, n, d_att), BF16),
        jax.ShapeDtypeStruct((b, n, d_att), BF16),
    ]
    if emit_cache:
        out_specs += [tok(), tok()]
        out_shape += [jax.ShapeDtypeStruct((b, n, d_att), F32)] * 2

    return pl.pallas_call(
        functools.partial(_proj_kernel, use_rope=use_rope, emit_cache=emit_cache,
                          append_cache=append_cache, q_tile=q_tile),
        grid=(b, n_keys // tm),
        in_specs=in_specs,
        out_specs=out_specs,
        out_shape=out_shape,
        compiler_params=pltpu.CompilerParams(
            dimension_semantics=("parallel", "arbitrary"),
            vmem_limit_bytes=VMEM_LIMIT),
        name="project_latent" if use_rope else "project_ctx",
    )(*args)


def _attn_kernel(lam_ref, qt_ref, k_ref, vt_ref, ga_ref, u_ref, up_ref, un_ref, gc_ref, x_ref,
                 gate_ref, sw_ref, cw_ref, cb_ref, wo_ref, out_ref,
                 s_ref, mx_a, mx_b, p_a, p_b, al_a, al_b, st_ref, worst_ref, acc_ref, cat_ref,
                 *, n_new, n_keys, lam_init):
    tq = x_ref.shape[1]
    n_heads = qt_ref.shape[1] // V_DIM
    d_att = n_heads * V_DIM
    lam = lam_ref[0:1, 0:1]

    def conv_branch():
        u = u_ref[0].astype(F32)
        t = pl.program_id(1)
        prev_row = up_ref[0].astype(F32)[BF16_SUBLANES - 1:BF16_SUBLANES, :]
        next_row = un_ref[0].astype(F32)[0:1, :]
        prev_row = jnp.where(t == 0, 0.0, prev_row)
        next_row = jnp.where(t == pl.num_programs(1) - 1, 0.0, next_row)
        rows = lax.broadcasted_iota(jnp.int32, u.shape, 0)
        u_prev = jnp.where(rows == 0, prev_row, pltpu.roll(u, 1, axis=0))
        u_next = jnp.where(rows == tq - 1, next_row, pltpu.roll(u, tq - 1, axis=0))
        conv = (u_prev * cw_ref[0:1, :] + u * cw_ref[1:2, :] + u_next * cw_ref[2:3, :]
                + cb_ref[...])
        conv = (gc_ref[0].astype(F32) * conv).astype(BF16)
        y_conv = jnp.dot(conv, wo_ref[d_att:, :], preferred_element_type=F32)
        out_ref[0] = x_ref[0] + gate_ref[0] * y_conv


    n_full = n_new // KEY_UNIT
    head_units = [(n_new, n_keys - n_new)] if n_keys > n_new else []
    if n_full == 0:
        head_units.append((0, n_new))
    n_units = len(head_units) + n_full

    def unit(i):
        if isinstance(i, int) and i < len(head_units):
            return head_units[i]
        off = (i - len(head_units)) * KEY_UNIT
        return (off if isinstance(i, int) else pl.multiple_of(off, KEY_UNIT)), KEY_UNIT

    def score_dot(unit, h):
        off, rows = unit
        hs = slice(h * V_DIM, (h + 1) * V_DIM)
        return jnp.dot(k_ref[0, pl.ds(off, rows), hs], qt_ref[0, hs, :],
                       preferred_element_type=F32)

    def group_max(s):
        return jnp.max(s.reshape(s.shape[0] // 8, 8, s.shape[1]), axis=0)

    def pv_dot(unit, h, p_ref):
        off, rows = unit
        return jnp.dot(vt_ref[0, h * V_EXT:(h + 1) * V_EXT, pl.ds(off, rows)],
                       p_ref[h, 0:rows, :], preferred_element_type=F32)

    def exact_scores(unit):
        for h in range(n_heads):
            s = score_dot(unit, h)
            s_ref[h, 0:unit[1], :] = s
            mx_a[h] = group_max(s)

    def exact_softmax(unit):
        rows = unit[1]
        for h in range(n_heads):
            m_old = st_ref[h:h + 1, :]
            m_new = jnp.maximum(m_old, jnp.max(mx_a[h], axis=0, keepdims=True))
            st_ref[h:h + 1, :] = m_new
            al_a[h:h + 1, :] = jnp.exp2(m_old - m_new)
            p_a[h, 0:rows, :] = jnp.exp2(s_ref[h, 0:rows, :] - m_new).astype(BF16)

    def exact_accumulate(unit, first):
        for h in range(n_heads):
            pv = pv_dot(unit, h, p_a)
            acc_ref[h] = pv if first else al_a[h:h + 1, :] * acc_ref[h] + pv

    def exact_unit(unit, first=False):
        exact_scores(unit)
        exact_softmax(unit)
        exact_accumulate(unit, first)

    def fast_scores(unit, buf):
        mx_ref, p_ref, _ = buf
        for h in range(n_heads):
            s = score_dot(unit, h)
            p_ref[h, 0:unit[1], :] = jnp.exp2(s - st_ref[h:h + 1, :]).astype(BF16)
            mx_ref[h] = group_max(s)

    def fast_rescale(buf):
        mx_ref, _, al_ref = buf
        for h in range(n_heads):
            m_prev = st_ref[h:h + 1, :]
            m_unit = jnp.max(mx_ref[h], axis=0, keepdims=True)
            m_new = jnp.maximum(m_prev, m_unit)
            st_ref[h:h + 1, :] = m_new
            al_ref[h:h + 1, :] = jnp.exp2(m_prev - m_new)
            worst_ref[h:h + 1, :] = jnp.maximum(worst_ref[h:h + 1, :], m_unit - m_prev)

    def fast_accumulate(unit, buf):
        _, p_ref, al_ref = buf
        for h in range(n_heads):
            acc_ref[h] = al_ref[h:h + 1, :] * (acc_ref[h] + pv_dot(unit, h, p_ref))

    def finish():
        for h in range(n_heads):
            hs = slice(h * V_DIM, (h + 1) * V_DIM)
            l0 = acc_ref[h, V_DIM:V_DIM + 1, 0:tq]
            l1 = acc_ref[h, V_DIM:V_DIM + 1, tq:2 * tq]
            o_t = (acc_ref[h, 0:V_DIM, 0:tq] * (1.0 / l0)
                   - acc_ref[h, 0:V_DIM, tq:2 * tq] * (lam / l1))
            o = o_t.T
            ms = jnp.mean(o * o, axis=-1, keepdims=True)
            att = o * lax.rsqrt(ms + EPS) * sw_ref[...] * (1.0 - lam_init)
            cat_ref[:, hs] = (att * ga_ref[0, :, hs].astype(F32)).astype(BF16)
        y_att = jnp.dot(cat_ref[...], wo_ref[0:d_att, :], preferred_element_type=F32)
        out_ref[0] += gate_ref[0] * y_att

    st_ref[...] = jnp.full(st_ref.shape, -jnp.inf, F32)

    if n_units == 1:
        conv_branch()
        exact_unit(unit(0), first=True)
        finish()
    else:
        assert n_units % 2 == 1 and n_units >= 3 and len(head_units) <= 1
        buf_a = (mx_a, p_a, al_a)
        buf_b = (mx_b, p_b, al_b)
        worst_ref[...] = jnp.full(worst_ref.shape, -jnp.inf, F32)

        def half(u, new, old, prev_exact=False):
            fast_scores(unit(u), new)
            fast_rescale(new)
            if prev_exact:
                exact_accumulate(unit(u - 1), first=True)
            else:
                fast_accumulate(unit(u - 1), old)

        exact_scores(unit(0))
        exact_softmax(unit(0))
        half(1, buf_b, buf_a, prev_exact=True)

        def pair(jj, carry):
            half(2 * jj, buf_a, buf_b)
            half(2 * jj + 1, buf_b, buf_a)
            return carry

        lax.fori_loop(1, (n_units - 1) // 2, pair, 0)
        last = n_units - 1
        half(last, buf_a, buf_b)
        fast_accumulate(unit(last), buf_a)
        conv_branch()
        finish()

        @pl.when(jnp.max(worst_ref[0:n_heads, :]) > FAST_RANGE)
        def _():
            st_ref[...] = jnp.full(st_ref.shape, -jnp.inf, F32)
            exact_unit(unit(0), first=True)

            def redo(i, carry):
                exact_unit(unit(i))
                return carry

            lax.fori_loop(1, n_units, redo, 0)
            conv_branch()
            finish()


def _attend(lam, qt, k, vt, n_keys, ga, u, gc, x, mod3, mod_row_fn, subln_w, conv_w, conv_b,
            w_out, lam_init):
    b, n, d = x.shape
    d_att = k.shape[2]
    tq = min(TILE, n)
    nt = n // tq
    halo = BF16_SUBLANES
    blocks_per_tile = tq // halo
    n_halo_blocks = n // halo
    n_heads = d_att // V_DIM

    tok = lambda: pl.BlockSpec((1, tq, d_att), lambda i, t: (i, t, 0))
    in_specs = [
        pl.BlockSpec((8, LANE), lambda i, t: (0, 0)),
        pl.BlockSpec((1, d_att, 2 * tq), lambda i, t: (i, 0, t)),
        pl.BlockSpec((1, k.shape[1], d_att), lambda i, t: (i, 0, 0)),
        pl.BlockSpec((1, vt.shape[1], vt.shape[2]), lambda i, t: (i, 0, 0)),
        tok(),
        tok(),
        pl.BlockSpec((1, halo, d_att),
                     lambda i, t: (i, jnp.maximum(t * blocks_per_tile - 1, 0), 0)),
        pl.BlockSpec((1, halo, d_att),
                     lambda i, t: (i, jnp.minimum((t + 1) * blocks_per_tile, n_halo_blocks - 1), 0)),
        tok(),
        pl.BlockSpec((1, tq, d), lambda i, t: (i, t, 0)),
        pl.BlockSpec((1, 1, d), lambda i, t: (mod_row_fn(i), 0, 2)),
        pl.BlockSpec((1, V_DIM), lambda i, t: (0, 0)),
        pl.BlockSpec((CONV_W, d_att), lambda i, t: (0, 0)),
        pl.BlockSpec((1, d_att), lambda i, t: (0, 0)),
        pl.BlockSpec((2 * d_att, d), lambda i, t: (0, 0)),
    ]
    args = [lam, qt, k, vt, ga, u, u, u, gc, x, mod3, subln_w, conv_w, conv_b, w_out]
    unit_rows = min(KEY_UNIT, n_keys)
    score_buf = lambda: pltpu.VMEM((n_heads, unit_rows, 2 * tq), F32)
    smax_buf = lambda: pltpu.VMEM((n_heads, 8, 2 * tq), F32)
    prob_buf = lambda: pltpu.VMEM((n_heads, unit_rows, 2 * tq), BF16)
    alpha_buf = lambda: pltpu.VMEM((8, 2 * tq), F32)

    return pl.pallas_call(
        functools.partial(_attn_kernel, n_new=n, n_keys=n_keys, lam_init=lam_init),
        grid=(b, nt),
        in_specs=in_specs,
        out_specs=pl.BlockSpec((1, tq, d), lambda i, t: (i, t, 0)),
        out_shape=jax.ShapeDtypeStruct((b, n, d), F32),
        scratch_shapes=[
            score_buf(), smax_buf(), smax_buf(), prob_buf(), prob_buf(),
            alpha_buf(), alpha_buf(),
            alpha_buf(),
            alpha_buf(),
            pltpu.VMEM((n_heads, V_EXT, 2 * tq), F32),
            pltpu.VMEM((tq, d_att), BF16),
        ],
        compiler_params=pltpu.CompilerParams(
            dimension_semantics=("parallel", "arbitrary"),
            vmem_limit_bytes=VMEM_LIMIT),
        name="attend_latent" if n_keys > n else "attend_ctx",
    )(*args)


def _rope_tables_t(n):
    t = jnp.arange(n)
    row = (t // GRID_W).astype(F32)
    col = (t % GRID_W).astype(F32)
    inv = 1.0 / (ROPE_BASE ** (jnp.arange(ROPE_FREQS, dtype=F32) / ROPE_FREQS))
    ang_r = row[:, None] * inv
    ang_c = col[:, None] * inv
    return (jnp.cos(ang_r).T, jnp.sin(ang_r).T, jnp.cos(ang_c).T, jnp.sin(ang_c).T)


def kernel(x_prompt, x_sample, cache_k, cache_v, c, c_ctx, norm_w, w_ada, b_ada, w_in,
           q_norm_w, k_norm_w, lambda_q1, lambda_k1, lambda_q2, lambda_k2, subln_w,
           conv_w, conv_b, w_out):
    depth = norm_w.shape[0]
    assert depth == 1
    i = 0
    lam_init = 0.8 - 0.6 * math.exp(-0.3 * i)
    b_ctx, n_ctx, d = x_prompt.shape
    b_lat, n_lat, _ = x_sample.shape
    d_att = w_in.shape[2] // 8
    n_heads = d_att // V_DIM

    n_rows = 16
    cvecs = jnp.zeros((n_rows, d), F32).at[:b_lat].set(c).at[b_lat].set(c_ctx)
    mod, lam = _modulation(cvecs, w_ada[i], b_ada[i][None], lambda_q1[i][None], lambda_k1[i][None],
                           lambda_q2[i][None], lambda_k2[i][None], lam_init)
    mod3 = mod.reshape(n_rows, 1, 3 * d)

    w_in_b = w_in[i].astype(BF16)
    w_out_b = w_out[i].astype(BF16)
    nw = norm_w[i][None]
    qw, kw = q_norm_w[i], k_norm_w[i]
    sw = subln_w[i][None]
    cw = conv_w[i]
    cb = conv_b[i][None]

    ctx_row = lambda bi: b_lat
    lat_row = lambda bi: bi

    qt, k, vt, ga, u, gc, kf, vf = _project(x_prompt, mod3, ctx_row, nw, w_in_b, qw, kw,
                                            None, True, None, min(TILE, n_ctx))
    y_prompt = _attend(lam, qt, k, vt, n_ctx, ga, u, gc, x_prompt, mod3, ctx_row, sw, cw, cb,
                       w_out_b, lam_init)
    new_cache_k = kf.reshape(b_ctx, 1, n_ctx, n_heads, 2, HEAD_DIM)
    new_cache_v = vf.reshape(b_ctx, 1, n_ctx, n_heads, V_DIM)

    past = cache_k.shape[2]
    cache = (cache_k[:, i].reshape(b_lat, past, d_att), cache_v[:, i].reshape(b_lat, past, d_att))
    qt, k, vt, ga, u, gc = _project(x_sample, mod3, lat_row, nw, w_in_b, qw, kw,
                                    _rope_tables_t(n_lat), False, cache, min(TILE, n_lat))
    y_sample = _attend(lam, qt, k, vt, n_lat + past, ga, u, gc, x_sample, mod3, lat_row, sw, cw, cb,
                       w_out_b, lam_init)
    return (y_prompt, y_sample, new_cache_k, new_cache_v)
```

```python
import functools
import math

import jax
import jax.numpy as jnp
from jax import lax
from jax.experimental import pallas as pl
from jax.experimental.pallas import tpu as pltpu

F32 = jnp.float32
BF16 = jnp.bfloat16

HEAD_DIM = 64
V_DIM = 2 * HEAD_DIM
GRID_W = 64
ROPE_FREQS = HEAD_DIM // 4
ROPE_BASE = 10000.0
EPS = 1e-6
CONV_W = 3
LANE = 128
BF16_SUBLANES = 16
VMEM_LIMIT = 56 * 1024 * 1024
TILE = 512
PROJ_TILE = 1024
KEY_UNIT = 256
FAST_RANGE = 64.0
Q_SCALE = HEAD_DIM ** -0.5 * math.log2(math.e)
V_EXT = V_DIM + BF16_SUBLANES


def _silu(x):
    return x * (1.0 / (1.0 + jnp.exp(-x)))


def _mod_kernel(c_ref, w_ref, b_ref, lq1, lk1, lq2, lk2, mod_ref, lam_ref, *, lam_init):
    a = _silu(c_ref[...]).astype(BF16)
    m = jnp.dot(a, w_ref[...].astype(BF16), preferred_element_type=F32)
    mod_ref[...] = m + b_ref[...]
    e1 = jnp.exp(jnp.sum(lq1[...] * lk1[...], axis=-1, keepdims=True))
    e2 = jnp.exp(jnp.sum(lq2[...] * lk2[...], axis=-1, keepdims=True))
    lam_ref[...] = jnp.broadcast_to(e1 - e2 + lam_init, lam_ref.shape)


def _modulation(cvecs, w_ada, b_ada, lq1, lk1, lq2, lk2, lam_init):
    rows, d = cvecs.shape
    n_out = w_ada.shape[1]
    bn = 512
    vec = lambda: pl.BlockSpec((1, HEAD_DIM), lambda j: (0, 0))
    return pl.pallas_call(
        functools.partial(_mod_kernel, lam_init=lam_init),
        grid=(n_out // bn,),
        in_specs=[
            pl.BlockSpec((rows, d), lambda j: (0, 0)),
            pl.BlockSpec((d, bn), lambda j: (0, j)),
            pl.BlockSpec((1, bn), lambda j: (0, j)),
            vec(), vec(), vec(), vec(),
        ],
        out_specs=[
            pl.BlockSpec((rows, bn), lambda j: (0, j)),
            pl.BlockSpec((8, LANE), lambda j: (0, 0)),
        ],
        out_shape=[
            jax.ShapeDtypeStruct((rows, n_out), F32),
            jax.ShapeDtypeStruct((8, LANE), F32),
        ],
        name="modulation",
    )(cvecs, w_ada, b_ada, lq1, lk1, lq2, lk2)


def _norm_rope_t(z_t, w_b, rope):
    outs = []
    f = ROPE_FREQS
    for g in range(z_t.shape[0] // HEAD_DIM):
        zg = z_t[g * HEAD_DIM:(g + 1) * HEAD_DIM, :]
        ms = jnp.mean(zg * zg, axis=0, keepdims=True)
        yg = zg * lax.rsqrt(ms + EPS) * w_b
        if rope is not None:
            cr, sr, cc, sc = rope
            x1r, x2r, x1c, x2c = yg[0:f], yg[f:2 * f], yg[2 * f:3 * f], yg[3 * f:4 * f]
            yg = jnp.concatenate([x1r * cr - x2r * sr, x2r * cr + x1r * sr,
                                  x1c * cc - x2c * sc, x2c * cc + x1c * sc], axis=0)
        outs.append(yg)
    return outs


def _store_vt(vt_ref, v_t):
    t = v_t.shape[1]
    ones_blk = (lax.broadcasted_iota(jnp.int32, (V_EXT - V_DIM, t), 0) == 0).astype(BF16)
    for h in range(v_t.shape[0] // V_DIM):
        vt_ref[0, h * V_EXT:h * V_EXT + V_DIM, 0:t] = v_t[h * V_DIM:(h + 1) * V_DIM].astype(BF16)
        vt_ref[0, h * V_EXT + V_DIM:(h + 1) * V_EXT, 0:t] = ones_blk


def _proj_kernel(*refs, use_rope, emit_cache, append_cache, q_tile):
    it = iter(refs)
    x_ref, shift_ref, scale_ref, nw_ref, w_ref, qw_ref, kw_ref = (next(it) for _ in range(7))
    rope_refs = tuple(next(it) for _ in range(4)) if use_rope else None
    if append_cache:
        ck_ref, cv_ref = next(it), next(it)
    qt_ref, k_ref, vt_ref, ga_ref, u_ref, gc_ref = (next(it) for _ in range(6))
    if emit_cache:
        kf_ref, vf_ref = next(it), next(it)

    def project():
        x = x_ref[0]
        ms = jnp.mean(x * x, axis=-1, keepdims=True)
        y = x * lax.rsqrt(ms + EPS) * nw_ref[...]
        h = (y * (1.0 + scale_ref[0]) + shift_ref[0]).astype(BF16)
        rope = tuple(r[...] for r in rope_refs) if use_rope else None
        d_att = k_ref.shape[2]

        def zcol(i):
            return jnp.dot(h, w_ref[:, i * d_att:(i + 1) * d_att], preferred_element_type=F32)

        q_groups = _norm_rope_t(zcol(0).T, qw_ref[...], rope)
        zero = jnp.zeros_like(q_groups[0])
        for g, yg in enumerate(q_groups):
            head, m = divmod(g, 2)
            yg = yg * Q_SCALE
            padded = jnp.concatenate([yg, zero] if m == 0 else [zero, yg], axis=0).astype(BF16)
            for j in range(padded.shape[1] // q_tile):
                col = (2 * j + m) * q_tile
                qt_ref[0, head * V_DIM:(head + 1) * V_DIM, col:col + q_tile] = (
                    padded[:, j * q_tile:(j + 1) * q_tile])

        k = jnp.concatenate(_norm_rope_t(zcol(1).T, kw_ref[...], rope), axis=0).T
        k_ref[0] = k.astype(BF16)
        v = zcol(2)
        _store_vt(vt_ref, v.T)
        if emit_cache:
            kf_ref[0] = k
            vf_ref[0] = v

        ga_ref[0] = _silu(zcol(3)).astype(BF16)
        cb = zcol(4)
        u_ref[0] = (zcol(5) * zcol(6)).astype(BF16)
        gc_ref[0] = (cb * _silu(zcol(7))).astype(BF16)

    if append_cache:
        last = pl.num_programs(1) - 1
        pl.when(pl.program_id(1) < last)(project)

        @pl.when(pl.program_id(1) == last)
        def _():
            past = ck_ref.shape[1]
            k_ref[0] = jnp.zeros(k_ref.shape[1:], BF16)
            vt_ref[0] = jnp.zeros(vt_ref.shape[1:], BF16)
            k_ref[0, 0:past, :] = ck_ref[0].astype(BF16)
            _store_vt(vt_ref, cv_ref[0].T)
    else:
        project()


def _project(x, mod3, mod_row_fn, norm_w, w_in, q_norm_w, k_norm_w, rope_t, emit_cache, cache,
             q_tile):
    b, n, d = x.shape
    d_in = w_in.shape[1]
    d_att = d_in // 8
    d_vt = d_att // V_DIM * V_EXT
    tm = min(PROJ_TILE, n)
    nt = n // tm
    use_rope = rope_t is not None
    append_cache = cache is not None
    n_keys = n
    tok_t = lambda t: t
    if append_cache:
        assert cache[0].shape[1] <= tm
        n_keys = n + tm
        tok_t = lambda t: jnp.minimum(t, nt - 1)
    qw_b = jnp.broadcast_to(q_norm_w[:, None], (HEAD_DIM, tm))
    kw_b = jnp.broadcast_to(k_norm_w[:, None], (HEAD_DIM, tm))

    in_specs = [
        pl.BlockSpec((1, tm, d), lambda i, t: (i, tok_t(t), 0)),
        pl.BlockSpec((1, 1, d), lambda i, t: (mod_row_fn(i), 0, 0)),
        pl.BlockSpec((1, 1, d), lambda i, t: (mod_row_fn(i), 0, 1)),
        pl.BlockSpec((1, d), lambda i, t: (0, 0)),
        pl.BlockSpec((d, d_in), lambda i, t: (0, 0)),
        pl.BlockSpec((HEAD_DIM, tm), lambda i, t: (0, 0)),
        pl.BlockSpec((HEAD_DIM, tm), lambda i, t: (0, 0)),
    ]
    args = [x, mod3, mod3, norm_w, w_in, qw_b, kw_b]
    if use_rope:
        in_specs += [pl.BlockSpec((ROPE_FREQS, tm), lambda i, t: (0, tok_t(t)))] * 4
        args += list(rope_t)
    if append_cache:
        in_specs += [pl.BlockSpec((1, cache[0].shape[1], d_att), lambda i, t: (i, 0, 0))] * 2
        args += list(cache)

    tok = lambda: pl.BlockSpec((1, tm, d_att), lambda i, t: (i, tok_t(t), 0))
    out_specs = [
        pl.BlockSpec((1, d_att, 2 * tm), lambda i, t: (i, 0, tok_t(t))),
        pl.BlockSpec((1, tm, d_att), lambda i, t: (i, t, 0)),
        pl.BlockSpec((1, d_vt, tm), lambda i, t: (i, 0, t)),
        tok(), tok(), tok(),
    ]
    out_shape = [
        jax.ShapeDtypeStruct((b, d_att, 2 * n), BF16),
        jax.ShapeDtypeStruct((b, n_keys, d_att), BF16),
        jax.ShapeDtypeStruct((b, d_vt, n_keys), BF16),
        jax.ShapeDtypeStruct((b, n, d_att), BF16),
        jax.ShapeDtypeStruct((b, n, d_att), BF16),
        jax.ShapeDtypeStruct((b, n, d_att), BF16),
    ]
    if emit_cache:
        out_specs += [tok(), tok()]
        out_shape += [jax.ShapeDtypeStruct((b, n, d_att), F32)] * 2

    return pl.pallas_call(
        functools.partial(_proj_kernel, use_rope=use_rope, emit_cache=emit_cache,
                          append_cache=append_cache, q_tile=q_tile),
        grid=(b, n_keys // tm),
        in_specs=in_specs,
        out_specs=out_specs,
        out_shape=out_shape,
        compiler_params=pltpu.CompilerParams(
            dimension_semantics=("parallel", "arbitrary"),
            vmem_limit_bytes=VMEM_LIMIT),
        name="project_latent" if use_rope else "project_ctx",
    )(*args)


def _attn_kernel(lam_ref, qt_ref, k_ref, vt_ref, ga_ref, u_ref, up_ref, un_ref, gc_ref, x_ref,
                 gate_ref, sw_ref, cw_ref, cb_ref, wo_ref, out_ref,
                 s_ref, mx_a, mx_b, p_a, p_b, al_a, al_b, st_ref, worst_ref, acc_ref, cat_ref,
                 *, n_new, n_keys, lam_init):
    tq = x_ref.shape[1]
    n_heads = qt_ref.shape[1] // V_DIM
    d_att = n_heads * V_DIM
    lam = lam_ref[0:1, 0:1]

    def conv_branch():
        u = u_ref[0].astype(F32)
        t = pl.program_id(1)
        prev_row = up_ref[0].astype(F32)[BF16_SUBLANES - 1:BF16_SUBLANES, :]
        next_row = un_ref[0].astype(F32)[0:1, :]
        prev_row = jnp.where(t == 0, 0.0, prev_row)
        next_row = jnp.where(t == pl.num_programs(1) - 1, 0.0, next_row)
        rows = lax.broadcasted_iota(jnp.int32, u.shape, 0)
        u_prev = jnp.where(rows == 0, prev_row, pltpu.roll(u, 1, axis=0))
        u_next = jnp.where(rows == tq - 1, next_row, pltpu.roll(u, tq - 1, axis=0))
        conv = (u_prev * cw_ref[0:1, :] + u * cw_ref[1:2, :] + u_next * cw_ref[2:3, :]
                + cb_ref[...])
        conv = (gc_ref[0].astype(F32) * conv).astype(BF16)
        y_conv = jnp.dot(conv, wo_ref[d_att:, :], preferred_element_type=F32)
        out_ref[0] = x_ref[0] + gate_ref[0] * y_conv


    n_full = n_new // KEY_UNIT
    head_units = [(n_new, n_keys - n_new)] if n_keys > n_new else []
    if n_full == 0:
        head_units.append((0, n_new))
    n_units = len(head_units) + n_full

    def unit(i):
        if isinstance(i, int) and i < len(head_units):
            return head_units[i]
        off = (i - len(head_units)) * KEY_UNIT
        return (off if isinstance(i, int) else pl.multiple_of(off, KEY_UNIT)), KEY_UNIT

    def score_dot(unit, h):
        off, rows = unit
        hs = slice(h * V_DIM, (h + 1) * V_DIM)
        return jnp.dot(k_ref[0, pl.ds(off, rows), hs], qt_ref[0, hs, :],
                       preferred_element_type=F32)

    def group_max(s):
        return jnp.max(s.reshape(s.shape[0] // 8, 8, s.shape[1]), axis=0)

    def pv_dot(unit, h, p_ref):
        off, rows = unit
        return jnp.dot(vt_ref[0, h * V_EXT:(h + 1) * V_EXT, pl.ds(off, rows)],
                       p_ref[h, 0:rows, :], preferred_element_type=F32)

    def exact_scores(unit):
        for h in range(n_heads):
            s = score_dot(unit, h)
            s_ref[h, 0:unit[1], :] = s
            mx_a[h] = group_max(s)

    def exact_softmax(unit):
        rows = unit[1]
        for h in range(n_heads):
            m_old = st_ref[h:h + 1, :]
            m_new = jnp.maximum(m_old, jnp.max(mx_a[h], axis=0, keepdims=True))
            st_ref[h:h + 1, :] = m_new
            al_a[h:h + 1, :] = jnp.exp2(m_old - m_new)
            p_a[h, 0:rows, :] = jnp.exp2(s_ref[h, 0:rows, :] - m_new).astype(BF16)

    def exact_accumulate(unit, first):
        for h in range(n_heads):
            pv = pv_dot(unit, h, p_a)
            acc_ref[h] = pv if first else al_a[h:h + 1, :] * acc_ref[h] + pv

    def exact_unit(unit, first=False):
        exact_scores(unit)
        exact_softmax(unit)
        exact_accumulate(unit, first)

    def fast_scores(unit, buf):
        mx_ref, p_ref, _ = buf
        for h in range(n_heads):
            s = score_dot(unit, h)
            p_ref[h, 0:unit[1], :] = jnp.exp2((s - st_ref[h:h + 1, :]).astype(BF16))
            mx_ref[h] = group_max(s)

    def fast_rescale(buf):
        mx_ref, _, al_ref = buf
        for h in range(n_heads):
            m_prev = st_ref[h:h + 1, :]
            m_unit = jnp.max(mx_ref[h], axis=0, keepdims=True)
            m_new = jnp.maximum(m_prev, m_unit)
            st_ref[h:h + 1, :] = m_new
            al_ref[h:h + 1, :] = jnp.exp2(m_prev - m_new)
            worst_ref[h:h + 1, :] = jnp.maximum(worst_ref[h:h + 1, :], m_unit - m_prev)

    def fast_accumulate(unit, buf):
        _, p_ref, al_ref = buf
        for h in range(n_heads):
            acc_ref[h] = al_ref[h:h + 1, :] * (acc_ref[h] + pv_dot(unit, h, p_ref))

    def finish():
        for h in range(n_heads):
            hs = slice(h * V_DIM, (h + 1) * V_DIM)
            l0 = acc_ref[h, V_DIM:V_DIM + 1, 0:tq]
            l1 = acc_ref[h, V_DIM:V_DIM + 1, tq:2 * tq]
            o_t = (acc_ref[h, 0:V_DIM, 0:tq] * (1.0 / l0)
                   - acc_ref[h, 0:V_DIM, tq:2 * tq] * (lam / l1))
            o = o_t.T
            ms = jnp.mean(o * o, axis=-1, keepdims=True)
            att = o * lax.rsqrt(ms + EPS) * sw_ref[...] * (1.0 - lam_init)
            cat_ref[:, hs] = (att * ga_ref[0, :, hs].astype(F32)).astype(BF16)
        y_att = jnp.dot(cat_ref[...], wo_ref[0:d_att, :], preferred_element_type=F32)
        out_ref[0] += gate_ref[0] * y_att

    st_ref[...] = jnp.full(st_ref.shape, -jnp.inf, F32)

    if n_units == 1:
        conv_branch()
        exact_unit(unit(0), first=True)
        finish()
    else:
        assert n_units % 2 == 1 and n_units >= 3 and len(head_units) <= 1
        buf_a = (mx_a, p_a, al_a)
        buf_b = (mx_b, p_b, al_b)
        worst_ref[...] = jnp.full(worst_ref.shape, -jnp.inf, F32)

        def half(u, new, old, prev_exact=False):
            fast_scores(unit(u), new)
            fast_rescale(new)
            if prev_exact:
                exact_accumulate(unit(u - 1), first=True)
            else:
                fast_accumulate(unit(u - 1), old)

        exact_scores(unit(0))
        exact_softmax(unit(0))
        half(1, buf_b, buf_a, prev_exact=True)

        def pair(jj, carry):
            half(2 * jj, buf_a, buf_b)
            half(2 * jj + 1, buf_b, buf_a)
            return carry

        lax.fori_loop(1, (n_units - 1) // 2, pair, 0)
        last = n_units - 1
        half(last, buf_a, buf_b)
        fast_accumulate(unit(last), buf_a)
        conv_branch()
        finish()

        @pl.when(jnp.max(worst_ref[0:n_heads, :]) > FAST_RANGE)
        def _():
            st_ref[...] = jnp.full(st_ref.shape, -jnp.inf, F32)
            exact_unit(unit(0), first=True)

            def redo(i, carry):
                exact_unit(unit(i))
                return carry

            lax.fori_loop(1, n_units, redo, 0)
            conv_branch()
            finish()


def _attend(lam, qt, k, vt, n_keys, ga, u, gc, x, mod3, mod_row_fn, subln_w, conv_w, conv_b,
            w_out, lam_init):
    b, n, d = x.shape
    d_att = k.shape[2]
    tq = min(TILE, n)
    nt = n // tq
    halo = BF16_SUBLANES
    blocks_per_tile = tq // halo
    n_halo_blocks = n // halo
    n_heads = d_att // V_DIM
    key_rows = -(-n_keys // LANE) * LANE
    assert key_rows <= k.shape[1]

    tok = lambda: pl.BlockSpec((1, tq, d_att), lambda i, t: (i, t, 0))
    in_specs = [
        pl.BlockSpec((8, LANE), lambda i, t: (0, 0)),
        pl.BlockSpec((1, d_att, 2 * tq), lambda i, t: (i, 0, t)),
        pl.BlockSpec((1, key_rows, d_att), lambda i, t: (i, 0, 0)),
        pl.BlockSpec((1, vt.shape[1], key_rows), lambda i, t: (i, 0, 0)),
        tok(),
        tok(),
        pl.BlockSpec((1, halo, d_att),
                     lambda i, t: (i, jnp.maximum(t * blocks_per_tile - 1, 0), 0)),
        pl.BlockSpec((1, halo, d_att),
                     lambda i, t: (i, jnp.minimum((t + 1) * blocks_per_tile, n_halo_blocks - 1), 0)),
        tok(),
        pl.BlockSpec((1, tq, d), lambda i, t: (i, t, 0)),
        pl.BlockSpec((1, 1, d), lambda i, t: (mod_row_fn(i), 0, 2)),
        pl.BlockSpec((1, V_DIM), lambda i, t: (0, 0)),
        pl.BlockSpec((CONV_W, d_att), lambda i, t: (0, 0)),
        pl.BlockSpec((1, d_att), lambda i, t: (0, 0)),
        pl.BlockSpec((2 * d_att, d), lambda i, t: (0, 0)),
    ]
    args = [lam, qt, k, vt, ga, u, u, u, gc, x, mod3, subln_w, conv_w, conv_b, w_out]
    unit_rows = min(KEY_UNIT, n_keys)
    score_buf = lambda: pltpu.VMEM((n_heads, unit_rows, 2 * tq), F32)
    smax_buf = lambda: pltpu.VMEM((n_heads, 8, 2 * tq), F32)
    prob_buf = lambda: pltpu.VMEM((n_heads, unit_rows, 2 * tq), BF16)
    alpha_buf = lambda: pltpu.VMEM((8, 2 * tq), F32)

    return pl.pallas_call(
        functools.partial(_attn_kernel, n_new=n, n_keys=n_keys, lam_init=lam_init),
        grid=(b, nt),
        in_specs=in_specs,
        out_specs=pl.BlockSpec((1, tq, d), lambda i, t: (i, t, 0)),
        out_shape=jax.ShapeDtypeStruct((b, n, d), F32),
        scratch_shapes=[
            score_buf(), smax_buf(), smax_buf(), prob_buf(), prob_buf(),
            alpha_buf(), alpha_buf(),
            alpha_buf(),
            alpha_buf(),
            pltpu.VMEM((n_heads, V_EXT, 2 * tq), F32),
            pltpu.VMEM((tq, d_att), BF16),
        ],
        compiler_params=pltpu.CompilerParams(
            dimension_semantics=("parallel", "arbitrary"),
            vmem_limit_bytes=VMEM_LIMIT),
        name="attend_latent" if n_keys > n else "attend_ctx",
    )(*args)


def _rope_tables_t(n):
    t = jnp.arange(n)
    row = (t // GRID_W).astype(F32)
    col = (t % GRID_W).astype(F32)
    inv = 1.0 / (ROPE_BASE ** (jnp.arange(ROPE_FREQS, dtype=F32) / ROPE_FREQS))
    ang_r = row[:, None] * inv
    ang_c = col[:, None] * inv
    return (jnp.cos(ang_r).T, jnp.sin(ang_r).T, jnp.cos(ang_c).T, jnp.sin(ang_c).T)


def kernel(x_prompt, x_sample, cache_k, cache_v, c, c_ctx, norm_w, w_ada, b_ada, w_in,
           q_norm_w, k_norm_w, lambda_q1, lambda_k1, lambda_q2, lambda_k2, subln_w,
           conv_w, conv_b, w_out):
    depth = norm_w.shape[0]
    assert depth == 1
    i = 0
    lam_init = 0.8 - 0.6 * math.exp(-0.3 * i)
    b_ctx, n_ctx, d = x_prompt.shape
    b_lat, n_lat, _ = x_sample.shape
    d_att = w_in.shape[2] // 8
    n_heads = d_att // V_DIM

    n_rows = 16
    cvecs = jnp.zeros((n_rows, d), F32).at[:b_lat].set(c).at[b_lat].set(c_ctx)
    mod, lam = _modulation(cvecs, w_ada[i], b_ada[i][None], lambda_q1[i][None], lambda_k1[i][None],
                           lambda_q2[i][None], lambda_k2[i][None], lam_init)
    mod3 = mod.reshape(n_rows, 1, 3 * d)

    w_in_b = w_in[i].astype(BF16)
    w_out_b = w_out[i].astype(BF16)
    nw = norm_w[i][None]
    qw, kw = q_norm_w[i], k_norm_w[i]
    sw = subln_w[i][None]
    cw = conv_w[i]
    cb = conv_b[i][None]

    ctx_row = lambda bi: b_lat
    lat_row = lambda bi: bi

    qt, k, vt, ga, u, gc, kf, vf = _project(x_prompt, mod3, ctx_row, nw, w_in_b, qw, kw,
                                            None, True, None, min(TILE, n_ctx))
    y_prompt = _attend(lam, qt, k, vt, n_ctx, ga, u, gc, x_prompt, mod3, ctx_row, sw, cw, cb,
                       w_out_b, lam_init)
    new_cache_k = kf.reshape(b_ctx, 1, n_ctx, n_heads, 2, HEAD_DIM)
    new_cache_v = vf.reshape(b_ctx, 1, n_ctx, n_heads, V_DIM)

    past = cache_k.shape[2]
    cache = (cache_k[:, i].reshape(b_lat, past, d_att), cache_v[:, i].reshape(b_lat, past, d_att))
    qt, k, vt, ga, u, gc = _project(x_sample, mod3, lat_row, nw, w_in_b, qw, kw,
                                    _rope_tables_t(n_lat), False, cache, min(TILE, n_lat))
    y_sample = _attend(lam, qt, k, vt, n_lat + past, ga, u, gc, x_sample, mod3, lat_row, sw, cw, cb,
                       w_out_b, lam_init)
    return (y_prompt, y_sample, new_cache_k, new_cache_v)
```

```python
import functools
import math

import jax
import jax.numpy as jnp
from jax import lax
from jax.experimental import pallas as pl
from jax.experimental.pallas import tpu as pltpu

F32 = jnp.float32
BF16 = jnp.bfloat16

HEAD_DIM = 64
V_DIM = 2 * HEAD_DIM
GRID_W = 64
ROPE_FREQS = HEAD_DIM // 4
ROPE_BASE = 10000.0
EPS = 1e-6
CONV_W = 3
LANE = 128
BF16_SUBLANES = 16
VMEM_LIMIT = 56 * 1024 * 1024
TILE = 512
PROJ_TILE = 1024
KEY_UNIT = 256
FAST_SUM_LIMIT = 2.0 ** 60
Q_SCALE = HEAD_DIM ** -0.5 * math.log2(math.e)
V_EXT = V_DIM + BF16_SUBLANES


def _silu(x):
    return x * (1.0 / (1.0 + jnp.exp(-x)))


def _mod_kernel(c_ref, w_ref, b_ref, lq1, lk1, lq2, lk2, mod_ref, lam_ref, *, lam_init):
    a = _silu(c_ref[...]).astype(BF16)
    m = jnp.dot(a, w_ref[...].astype(BF16), preferred_element_type=F32)
    mod_ref[...] = m + b_ref[...]
    e1 = jnp.exp(jnp.sum(lq1[...] * lk1[...], axis=-1, keepdims=True))
    e2 = jnp.exp(jnp.sum(lq2[...] * lk2[...], axis=-1, keepdims=True))
    lam_ref[...] = jnp.broadcast_to(e1 - e2 + lam_init, lam_ref.shape)


def _modulation(cvecs, w_ada, b_ada, lq1, lk1, lq2, lk2, lam_init):
    rows, d = cvecs.shape
    n_out = w_ada.shape[1]
    bn = 512
    vec = lambda: pl.BlockSpec((1, HEAD_DIM), lambda j: (0, 0))
    return pl.pallas_call(
        functools.partial(_mod_kernel, lam_init=lam_init),
        grid=(n_out // bn,),
        in_specs=[
            pl.BlockSpec((rows, d), lambda j: (0, 0)),
            pl.BlockSpec((d, bn), lambda j: (0, j)),
            pl.BlockSpec((1, bn), lambda j: (0, j)),
            vec(), vec(), vec(), vec(),
        ],
        out_specs=[
            pl.BlockSpec((rows, bn), lambda j: (0, j)),
            pl.BlockSpec((8, LANE), lambda j: (0, 0)),
        ],
        out_shape=[
            jax.ShapeDtypeStruct((rows, n_out), F32),
            jax.ShapeDtypeStruct((8, LANE), F32),
        ],
        name="modulation",
    )(cvecs, w_ada, b_ada, lq1, lk1, lq2, lk2)


def _norm_rope_t(z_t, w_b, rope):
    outs = []
    f = ROPE_FREQS
    for g in range(z_t.shape[0] // HEAD_DIM):
        zg = z_t[g * HEAD_DIM:(g + 1) * HEAD_DIM, :]
        ms = jnp.mean(zg * zg, axis=0, keepdims=True)
        yg = zg * lax.rsqrt(ms + EPS) * w_b
        if rope is not None:
            cr, sr, cc, sc = rope
            x1r, x2r, x1c, x2c = yg[0:f], yg[f:2 * f], yg[2 * f:3 * f], yg[3 * f:4 * f]
            yg = jnp.concatenate([x1r * cr - x2r * sr, x2r * cr + x1r * sr,
                                  x1c * cc - x2c * sc, x2c * cc + x1c * sc], axis=0)
        outs.append(yg)
    return outs


def _store_vt(vt_ref, v_t):
    t = v_t.shape[1]
    ones_blk = (lax.broadcasted_iota(jnp.int32, (V_EXT - V_DIM, t), 0) == 0).astype(BF16)
    for h in range(v_t.shape[0] // V_DIM):
        vt_ref[0, h * V_EXT:h * V_EXT + V_DIM, 0:t] = v_t[h * V_DIM:(h + 1) * V_DIM].astype(BF16)
        vt_ref[0, h * V_EXT + V_DIM:(h + 1) * V_EXT, 0:t] = ones_blk


def _proj_kernel(*refs, use_rope, emit_cache, append_cache, q_tile):
    it = iter(refs)
    x_ref, shift_ref, scale_ref, nw_ref, w_ref, qw_ref, kw_ref = (next(it) for _ in range(7))
    rope_refs = tuple(next(it) for _ in range(4)) if use_rope else None
    if append_cache:
        ck_ref, cv_ref = next(it), next(it)
    qt_ref, k_ref, vt_ref, ga_ref, u_ref, gc_ref = (next(it) for _ in range(6))
    if emit_cache:
        kf_ref, vf_ref = next(it), next(it)

    def project():
        x = x_ref[0]
        ms = jnp.mean(x * x, axis=-1, keepdims=True)
        y = x * lax.rsqrt(ms + EPS) * nw_ref[...]
        h = (y * (1.0 + scale_ref[0]) + shift_ref[0]).astype(BF16)
        rope = tuple(r[...] for r in rope_refs) if use_rope else None
        d_att = k_ref.shape[2]

        def zcol(i):
            return jnp.dot(h, w_ref[:, i * d_att:(i + 1) * d_att], preferred_element_type=F32)

        q_groups = _norm_rope_t(zcol(0).T, qw_ref[...], rope)
        zero = jnp.zeros_like(q_groups[0])
        for g, yg in enumerate(q_groups):
            head, m = divmod(g, 2)
            yg = yg * Q_SCALE
            padded = jnp.concatenate([yg, zero] if m == 0 else [zero, yg], axis=0).astype(BF16)
            for j in range(padded.shape[1] // q_tile):
                col = (2 * j + m) * q_tile
                qt_ref[0, head * V_DIM:(head + 1) * V_DIM, col:col + q_tile] = (
                    padded[:, j * q_tile:(j + 1) * q_tile])

        k = jnp.concatenate(_norm_rope_t(zcol(1).T, kw_ref[...], rope), axis=0).T
        k_ref[0] = k.astype(BF16)
        v = zcol(2)
        _store_vt(vt_ref, v.T)
        if emit_cache:
            kf_ref[0] = k
            vf_ref[0] = v

        ga_ref[0] = _silu(zcol(3)).astype(BF16)
        cb = zcol(4)
        u_ref[0] = (zcol(5) * zcol(6)).astype(BF16)
        gc_ref[0] = (cb * _silu(zcol(7))).astype(BF16)

    if append_cache:
        last = pl.num_programs(1) - 1
        pl.when(pl.program_id(1) < last)(project)

        @pl.when(pl.program_id(1) == last)
        def _():
            past = ck_ref.shape[1]
            k_ref[0] = jnp.zeros(k_ref.shape[1:], BF16)
            vt_ref[0] = jnp.zeros(vt_ref.shape[1:], BF16)
            k_ref[0, 0:past, :] = ck_ref[0].astype(BF16)
            _store_vt(vt_ref, cv_ref[0].T)
    else:
        project()


def _project(x, mod3, mod_row_fn, norm_w, w_in, q_norm_w, k_norm_w, rope_t, emit_cache, cache,
             q_tile):
    b, n, d = x.shape
    d_in = w_in.shape[1]
    d_att = d_in // 8
    d_vt = d_att // V_DIM * V_EXT
    tm = min(PROJ_TILE, n)
    nt = n // tm
    use_rope = rope_t is not None
    append_cache = cache is not None
    n_keys = n
    tok_t = lambda t: t
    if append_cache:
        assert cache[0].shape[1] <= tm
        n_keys = n + tm
        tok_t = lambda t: jnp.minimum(t, nt - 1)
    qw_b = jnp.broadcast_to(q_norm_w[:, None], (HEAD_DIM, tm))
    kw_b = jnp.broadcast_to(k_norm_w[:, None], (HEAD_DIM, tm))

    in_specs = [
        pl.BlockSpec((1, tm, d), lambda i, t: (i, tok_t(t), 0)),
        pl.BlockSpec((1, 1, d), lambda i, t: (mod_row_fn(i), 0, 0)),
        pl.BlockSpec((1, 1, d), lambda i, t: (mod_row_fn(i), 0, 1)),
        pl.BlockSpec((1, d), lambda i, t: (0, 0)),
        pl.BlockSpec((d, d_in), lambda i, t: (0, 0)),
        pl.BlockSpec((HEAD_DIM, tm), lambda i, t: (0, 0)),
        pl.BlockSpec((HEAD_DIM, tm), lambda i, t: (0, 0)),
    ]
    args = [x, mod3, mod3, norm_w, w_in, qw_b, kw_b]
    if use_rope:
        in_specs += [pl.BlockSpec((ROPE_FREQS, tm), lambda i, t: (0, tok_t(t)))] * 4
        args += list(rope_t)
    if append_cache:
        in_specs += [pl.BlockSpec((1, cache[0].shape[1], d_att), lambda i, t: (i, 0, 0))] * 2
        args += list(cache)

    tok = lambda: pl.BlockSpec((1, tm, d_att), lambda i, t: (i, tok_t(t), 0))
    out_specs = [
        pl.BlockSpec((1, d_att, 2 * tm), lambda i, t: (i, 0, tok_t(t))),
        pl.BlockSpec((1, tm, d_att), lambda i, t: (i, t, 0)),
        pl.BlockSpec((1, d_vt, tm), lambda i, t: (i, 0, t)),
        tok(), tok(), tok(),
    ]
    out_shape = [
        jax.ShapeDtypeStruct((b, d_att, 2 * n), BF16),
        jax.ShapeDtypeStruct((b, n_keys, d_att), BF16),
        jax.ShapeDtypeStruct((b, d_vt, n_keys), BF16),
        jax.ShapeDtypeStruct((b, n, d_att), BF16),
        jax.ShapeDtypeStruct((b, n, d_att), BF16),
        jax.ShapeDtypeStruct((b, n, d_att), BF16),
    ]
    if emit_cache:
        out_specs += [tok(), tok()]
        out_shape += [jax.ShapeDtypeStruct((b, n, d_att), F32)] * 2

    return pl.pallas_call(
        functools.partial(_proj_kernel, use_rope=use_rope, emit_cache=emit_cache,
                          append_cache=append_cache, q_tile=q_tile),
        grid=(b, n_keys // tm),
        in_specs=in_specs,
        out_specs=out_specs,
        out_shape=out_shape,
        compiler_params=pltpu.CompilerParams(
            dimension_semantics=("parallel", "arbitrary"),
            vmem_limit_bytes=VMEM_LIMIT),
        name="project_latent" if use_rope else "project_ctx",
    )(*args)


def _attn_kernel(lam_ref, qt_ref, k_ref, vt_ref, ga_ref, u_ref, up_ref, un_ref, gc_ref, x_ref,
                 gate_ref, sw_ref, cw_ref, cb_ref, wo_ref, out_ref,
                 s_ref, mx_a, p_a, p_b, al_a, st_ref, acc_ref, cat_ref,
                 *, n_new, n_keys, lam_init):
    tq = x_ref.shape[1]
    n_heads = qt_ref.shape[1] // V_DIM
    d_att = n_heads * V_DIM
    lam = lam_ref[0:1, 0:1]

    def conv_branch():
        u = u_ref[0].astype(F32)
        t = pl.program_id(1)
        prev_row = up_ref[0].astype(F32)[BF16_SUBLANES - 1:BF16_SUBLANES, :]
        next_row = un_ref[0].astype(F32)[0:1, :]
        prev_row = jnp.where(t == 0, 0.0, prev_row)
        next_row = jnp.where(t == pl.num_programs(1) - 1, 0.0, next_row)
        rows = lax.broadcasted_iota(jnp.int32, u.shape, 0)
        u_prev = jnp.where(rows == 0, prev_row, pltpu.roll(u, 1, axis=0))
        u_next = jnp.where(rows == tq - 1, next_row, pltpu.roll(u, tq - 1, axis=0))
        conv = (u_prev * cw_ref[0:1, :] + u * cw_ref[1:2, :] + u_next * cw_ref[2:3, :]
                + cb_ref[...])
        conv = (gc_ref[0].astype(F32) * conv).astype(BF16)
        y_conv = jnp.dot(conv, wo_ref[d_att:, :], preferred_element_type=F32)
        out_ref[0] = x_ref[0] + gate_ref[0] * y_conv


    n_full = n_new // KEY_UNIT
    head_units = [(n_new, n_keys - n_new)] if n_keys > n_new else []
    if n_full == 0:
        head_units.append((0, n_new))
    n_units = len(head_units) + n_full

    def unit(i):
        if isinstance(i, int) and i < len(head_units):
            return head_units[i]
        off = (i - len(head_units)) * KEY_UNIT
        return (off if isinstance(i, int) else pl.multiple_of(off, KEY_UNIT)), KEY_UNIT

    def score_dot(unit, h):
        off, rows = unit
        hs = slice(h * V_DIM, (h + 1) * V_DIM)
        return jnp.dot(k_ref[0, pl.ds(off, rows), hs], qt_ref[0, hs, :],
                       preferred_element_type=F32)

    def group_max(s):
        return jnp.max(s.reshape(s.shape[0] // 8, 8, s.shape[1]), axis=0)

    def pv_dot(unit, h, p_ref):
        off, rows = unit
        return jnp.dot(vt_ref[0, h * V_EXT:(h + 1) * V_EXT, pl.ds(off, rows)],
                       p_ref[h, 0:rows, :], preferred_element_type=F32)

    def exact_scores(unit):
        for h in range(n_heads):
            s = score_dot(unit, h)
            s_ref[h, 0:unit[1], :] = s
            mx_a[h] = group_max(s)

    def exact_softmax(unit):
        rows = unit[1]
        for h in range(n_heads):
            m_old = st_ref[h:h + 1, :]
            m_new = jnp.maximum(m_old, jnp.max(mx_a[h], axis=0, keepdims=True))
            st_ref[h:h + 1, :] = m_new
            al_a[h:h + 1, :] = jnp.exp2(m_old - m_new)
            p_a[h, 0:rows, :] = jnp.exp2(s_ref[h, 0:rows, :] - m_new).astype(BF16)

    def exact_accumulate(unit, first):
        for h in range(n_heads):
            pv = pv_dot(unit, h, p_a)
            acc_ref[h] = pv if first else al_a[h:h + 1, :] * acc_ref[h] + pv

    def exact_unit(unit, first=False):
        exact_scores(unit)
        exact_softmax(unit)
        exact_accumulate(unit, first)

    def fast_scores(unit, p_ref):
        for h in range(n_heads):
            s = score_dot(unit, h)
            p_ref[h, 0:unit[1], :] = jnp.exp2(s - st_ref[h:h + 1, :]).astype(BF16)

    def fast_accumulate(unit, p_ref):
        for h in range(n_heads):
            acc_ref[h] += pv_dot(unit, h, p_ref)

    def finish():
        bad = jnp.zeros((1, tq), F32)
        for h in range(n_heads):
            hs = slice(h * V_DIM, (h + 1) * V_DIM)
            l0 = acc_ref[h, V_DIM:V_DIM + 1, 0:tq]
            l1 = acc_ref[h, V_DIM:V_DIM + 1, tq:2 * tq]
            o_t = (acc_ref[h, 0:V_DIM, 0:tq] * (1.0 / l0)
                   - acc_ref[h, 0:V_DIM, tq:2 * tq] * (lam / l1))
            bad = (bad + jnp.where(l0 <= FAST_SUM_LIMIT, 0.0, 1.0)
                   + jnp.where(l1 <= FAST_SUM_LIMIT, 0.0, 1.0)
                   + jnp.max(jnp.where(jnp.abs(o_t) < jnp.inf, 0.0, 1.0), axis=0, keepdims=True))
            o = o_t.T
            ms = jnp.mean(o * o, axis=-1, keepdims=True)
            att = o * lax.rsqrt(ms + EPS) * sw_ref[...] * (1.0 - lam_init)
            cat_ref[:, hs] = (att * ga_ref[0, :, hs].astype(F32)).astype(BF16)
        y_att = jnp.dot(cat_ref[...], wo_ref[0:d_att, :], preferred_element_type=F32)
        out_ref[0] += gate_ref[0] * y_att
        return jnp.max(bad)

    st_ref[...] = jnp.full(st_ref.shape, -jnp.inf, F32)

    if n_units == 1:
        conv_branch()
        exact_unit(unit(0), first=True)
        finish()
    else:
        assert n_units % 2 == 1 and n_units >= 3 and len(head_units) <= 1

        def half(u, new, old, prev_exact=False):
            fast_scores(unit(u), new)
            if prev_exact:
                exact_accumulate(unit(u - 1), first=True)
            else:
                fast_accumulate(unit(u - 1), old)

        exact_scores(unit(0))
        exact_softmax(unit(0))
        half(1, p_b, p_a, prev_exact=True)

        def pair(jj, carry):
            half(2 * jj, p_a, p_b)
            half(2 * jj + 1, p_b, p_a)
            return carry

        lax.fori_loop(1, (n_units - 1) // 2, pair, 0)
        last = n_units - 1
        half(last, p_a, p_b)
        fast_accumulate(unit(last), p_a)
        conv_branch()
        out_of_range = finish()

        @pl.when(out_of_range > 0.0)
        def _():
            st_ref[...] = jnp.full(st_ref.shape, -jnp.inf, F32)
            exact_unit(unit(0), first=True)

            def redo(i, carry):
                exact_unit(unit(i))
                return carry

            lax.fori_loop(1, n_units, redo, 0)
            conv_branch()
            finish()


def _attend(lam, qt, k, vt, n_keys, ga, u, gc, x, mod3, mod_row_fn, subln_w, conv_w, conv_b,
            w_out, lam_init):
    b, n, d = x.shape
    d_att = k.shape[2]
    tq = min(TILE, n)
    nt = n // tq
    halo = BF16_SUBLANES
    blocks_per_tile = tq // halo
    n_halo_blocks = n // halo
    n_heads = d_att // V_DIM
    key_rows = -(-n_keys // LANE) * LANE
    assert key_rows <= k.shape[1]

    tok = lambda: pl.BlockSpec((1, tq, d_att), lambda i, t: (i, t, 0))
    in_specs = [
        pl.BlockSpec((8, LANE), lambda i, t: (0, 0)),
        pl.BlockSpec((1, d_att, 2 * tq), lambda i, t: (i, 0, t)),
        pl.BlockSpec((1, key_rows, d_att), lambda i, t: (i, 0, 0)),
        pl.BlockSpec((1, vt.shape[1], key_rows), lambda i, t: (i, 0, 0)),
        tok(),
        tok(),
        pl.BlockSpec((1, halo, d_att),
                     lambda i, t: (i, jnp.maximum(t * blocks_per_tile - 1, 0), 0)),
        pl.BlockSpec((1, halo, d_att),
                     lambda i, t: (i, jnp.minimum((t + 1) * blocks_per_tile, n_halo_blocks - 1), 0)),
        tok(),
        pl.BlockSpec((1, tq, d), lambda i, t: (i, t, 0)),
        pl.BlockSpec((1, 1, d), lambda i, t: (mod_row_fn(i), 0, 2)),
        pl.BlockSpec((1, V_DIM), lambda i, t: (0, 0)),
        pl.BlockSpec((CONV_W, d_att), lambda i, t: (0, 0)),
        pl.BlockSpec((1, d_att), lambda i, t: (0, 0)),
        pl.BlockSpec((2 * d_att, d), lambda i, t: (0, 0)),
    ]
    args = [lam, qt, k, vt, ga, u, u, u, gc, x, mod3, subln_w, conv_w, conv_b, w_out]
    unit_rows = min(KEY_UNIT, n_keys)
    score_buf = lambda: pltpu.VMEM((n_heads, unit_rows, 2 * tq), F32)
    smax_buf = lambda: pltpu.VMEM((n_heads, 8, 2 * tq), F32)
    prob_buf = lambda: pltpu.VMEM((n_heads, unit_rows, 2 * tq), BF16)
    alpha_buf = lambda: pltpu.VMEM((8, 2 * tq), F32)

    return pl.pallas_call(
        functools.partial(_attn_kernel, n_new=n, n_keys=n_keys, lam_init=lam_init),
        grid=(b, nt),
        in_specs=in_specs,
        out_specs=pl.BlockSpec((1, tq, d), lambda i, t: (i, t, 0)),
        out_shape=jax.ShapeDtypeStruct((b, n, d), F32),
        scratch_shapes=[
            score_buf(), smax_buf(), prob_buf(), prob_buf(),
            alpha_buf(),
            alpha_buf(),
            pltpu.VMEM((n_heads, V_EXT, 2 * tq), F32),
            pltpu.VMEM((tq, d_att), BF16),
        ],
        compiler_params=pltpu.CompilerParams(
            dimension_semantics=("parallel", "arbitrary"),
            vmem_limit_bytes=VMEM_LIMIT),
        name="attend_latent" if n_keys > n else "attend_ctx",
    )(*args)


def _rope_tables_t(n):
    t = jnp.arange(n)
    row = (t // GRID_W).astype(F32)
    col = (t % GRID_W).astype(F32)
    inv = 1.0 / (ROPE_BASE ** (jnp.arange(ROPE_FREQS, dtype=F32) / ROPE_FREQS))
    ang_r = row[:, None] * inv
    ang_c = col[:, None] * inv
    return (jnp.cos(ang_r).T, jnp.sin(ang_r).T, jnp.cos(ang_c).T, jnp.sin(ang_c).T)


def kernel(x_prompt, x_sample, cache_k, cache_v, c, c_ctx, norm_w, w_ada, b_ada, w_in,
           q_norm_w, k_norm_w, lambda_q1, lambda_k1, lambda_q2, lambda_k2, subln_w,
           conv_w, conv_b, w_out):
    depth = norm_w.shape[0]
    assert depth == 1
    i = 0
    lam_init = 0.8 - 0.6 * math.exp(-0.3 * i)
    b_ctx, n_ctx, d = x_prompt.shape
    b_lat, n_lat, _ = x_sample.shape
    d_att = w_in.shape[2] // 8
    n_heads = d_att // V_DIM

    n_rows = 16
    cvecs = jnp.zeros((n_rows, d), F32).at[:b_lat].set(c).at[b_lat].set(c_ctx)
    mod, lam = _modulation(cvecs, w_ada[i], b_ada[i][None], lambda_q1[i][None], lambda_k1[i][None],
                           lambda_q2[i][None], lambda_k2[i][None], lam_init)
    mod3 = mod.reshape(n_rows, 1, 3 * d)

    w_in_b = w_in[i].astype(BF16)
    w_out_b = w_out[i].astype(BF16)
    nw = norm_w[i][None]
    qw, kw = q_norm_w[i], k_norm_w[i]
    sw = subln_w[i][None]
    cw = conv_w[i]
    cb = conv_b[i][None]

    ctx_row = lambda bi: b_lat
    lat_row = lambda bi: bi

    qt, k, vt, ga, u, gc, kf, vf = _project(x_prompt, mod3, ctx_row, nw, w_in_b, qw, kw,
                                            None, True, None, min(TILE, n_ctx))
    y_prompt = _attend(lam, qt, k, vt, n_ctx, ga, u, gc, x_prompt, mod3, ctx_row, sw, cw, cb,
                       w_out_b, lam_init)
    new_cache_k = kf.reshape(b_ctx, 1, n_ctx, n_heads, 2, HEAD_DIM)
    new_cache_v = vf.reshape(b_ctx, 1, n_ctx, n_heads, V_DIM)

    past = cache_k.shape[2]
    cache = (cache_k[:, i].reshape(b_lat, past, d_att), cache_v[:, i].reshape(b_lat, past, d_att))
    qt, k, vt, ga, u, gc = _project(x_sample, mod3, lat_row, nw, w_in_b, qw, kw,
                                    _rope_tables_t(n_lat), False, cache, min(TILE, n_lat))
    y_sample = _attend(lam, qt, k, vt, n_lat + past, ga, u, gc, x_sample, mod3, lat_row, sw, cw, cb,
                       w_out_b, lam_init)
    return (y_prompt, y_sample, new_cache_k, new_cache_v)
```

```python
import functools
import math

import jax
import jax.numpy as jnp
from jax import lax
from jax.experimental import pallas as pl
from jax.experimental.pallas import tpu as pltpu

F32 = jnp.float32
BF16 = jnp.bfloat16

HEAD_DIM = 64
V_DIM = 2 * HEAD_DIM
GRID_W = 64
ROPE_FREQS = HEAD_DIM // 4
ROPE_BASE = 10000.0
EPS = 1e-6
CONV_W = 3
LANE = 128
BF16_SUBLANES = 16
VMEM_LIMIT = 56 * 1024 * 1024
TILE = 512
PROJ_TILE = 1024
KEY_UNIT = 256
FAST_SUM_LIMIT = 2.0 ** 60
Q_SCALE = HEAD_DIM ** -0.5 * math.log2(math.e)
V_EXT = V_DIM + BF16_SUBLANES


def _silu(x):
    return x * (1.0 / (1.0 + jnp.exp(-x)))


def _mod_kernel(c_ref, w_ref, b_ref, lq1, lk1, lq2, lk2, mod_ref, lam_ref, *, lam_init):
    a = _silu(c_ref[...]).astype(BF16)
    m = jnp.dot(a, w_ref[...].astype(BF16), preferred_element_type=F32)
    mod_ref[...] = m + b_ref[...]
    e1 = jnp.exp(jnp.sum(lq1[...] * lk1[...], axis=-1, keepdims=True))
    e2 = jnp.exp(jnp.sum(lq2[...] * lk2[...], axis=-1, keepdims=True))
    lam_ref[...] = jnp.broadcast_to(e1 - e2 + lam_init, lam_ref.shape)


def _modulation(cvecs, w_ada, b_ada, lq1, lk1, lq2, lk2, lam_init):
    rows, d = cvecs.shape
    n_out = w_ada.shape[1]
    bn = 512
    vec = lambda: pl.BlockSpec((1, HEAD_DIM), lambda j: (0, 0))
    return pl.pallas_call(
        functools.partial(_mod_kernel, lam_init=lam_init),
        grid=(n_out // bn,),
        in_specs=[
            pl.BlockSpec((rows, d), lambda j: (0, 0)),
            pl.BlockSpec((d, bn), lambda j: (0, j)),
            pl.BlockSpec((1, bn), lambda j: (0, j)),
            vec(), vec(), vec(), vec(),
        ],
        out_specs=[
            pl.BlockSpec((rows, bn), lambda j: (0, j)),
            pl.BlockSpec((8, LANE), lambda j: (0, 0)),
        ],
        out_shape=[
            jax.ShapeDtypeStruct((rows, n_out), F32),
            jax.ShapeDtypeStruct((8, LANE), F32),
        ],
        name="modulation",
    )(cvecs, w_ada, b_ada, lq1, lk1, lq2, lk2)


def _norm_rope_t(z_t, w_b, rope):
    outs = []
    f = ROPE_FREQS
    for g in range(z_t.shape[0] // HEAD_DIM):
        zg = z_t[g * HEAD_DIM:(g + 1) * HEAD_DIM, :]
        ms = jnp.mean(zg * zg, axis=0, keepdims=True)
        yg = zg * lax.rsqrt(ms + EPS) * w_b
        if rope is not None:
            cr, sr, cc, sc = rope
            x1r, x2r, x1c, x2c = yg[0:f], yg[f:2 * f], yg[2 * f:3 * f], yg[3 * f:4 * f]
            yg = jnp.concatenate([x1r * cr - x2r * sr, x2r * cr + x1r * sr,
                                  x1c * cc - x2c * sc, x2c * cc + x1c * sc], axis=0)
        outs.append(yg)
    return outs


def _store_vt(vt_ref, v_t):
    t = v_t.shape[1]
    ones_blk = (lax.broadcasted_iota(jnp.int32, (V_EXT - V_DIM, t), 0) == 0).astype(BF16)
    for h in range(v_t.shape[0] // V_DIM):
        vt_ref[0, h * V_EXT:h * V_EXT + V_DIM, 0:t] = v_t[h * V_DIM:(h + 1) * V_DIM].astype(BF16)
        vt_ref[0, h * V_EXT + V_DIM:(h + 1) * V_EXT, 0:t] = ones_blk


def _proj_kernel(*refs, use_rope, emit_cache, append_cache, q_tile):
    it = iter(refs)
    x_ref, shift_ref, scale_ref, nw_ref, w_ref, qw_ref, kw_ref = (next(it) for _ in range(7))
    rope_refs = tuple(next(it) for _ in range(4)) if use_rope else None
    if append_cache:
        ck_ref, cv_ref = next(it), next(it)
    qt_ref, k_ref, vt_ref, ga_ref, u_ref, gc_ref = (next(it) for _ in range(6))
    if emit_cache:
        kf_ref, vf_ref = next(it), next(it)

    def project():
        x = x_ref[0]
        ms = jnp.mean(x * x, axis=-1, keepdims=True)
        y = x * lax.rsqrt(ms + EPS) * nw_ref[...]
        h = (y * (1.0 + scale_ref[0]) + shift_ref[0]).astype(BF16)
        rope = tuple(r[...] for r in rope_refs) if use_rope else None
        d_att = k_ref.shape[2]

        def zcol(i):
            return jnp.dot(h, w_ref[:, i * d_att:(i + 1) * d_att], preferred_element_type=F32)

        q_groups = _norm_rope_t(zcol(0).T, qw_ref[...], rope)
        zero = jnp.zeros_like(q_groups[0])
        for g, yg in enumerate(q_groups):
            head, m = divmod(g, 2)
            yg = yg * Q_SCALE
            padded = jnp.concatenate([yg, zero] if m == 0 else [zero, yg], axis=0).astype(BF16)
            for j in range(padded.shape[1] // q_tile):
                col = (2 * j + m) * q_tile
                qt_ref[0, head * V_DIM:(head + 1) * V_DIM, col:col + q_tile] = (
                    padded[:, j * q_tile:(j + 1) * q_tile])

        k = jnp.concatenate(_norm_rope_t(zcol(1).T, kw_ref[...], rope), axis=0).T
        k_ref[0] = k.astype(BF16)
        v = zcol(2)
        _store_vt(vt_ref, v.T)
        if emit_cache:
            kf_ref[0] = k
            vf_ref[0] = v

        ga_ref[0] = _silu(zcol(3)).astype(BF16)
        cb = zcol(4)
        u_ref[0] = (zcol(5) * zcol(6)).astype(BF16)
        gc_ref[0] = (cb * _silu(zcol(7))).astype(BF16)

    if append_cache:
        last = pl.num_programs(1) - 1
        pl.when(pl.program_id(1) < last)(project)

        @pl.when(pl.program_id(1) == last)
        def _():
            past = ck_ref.shape[1]
            k_ref[0] = jnp.zeros(k_ref.shape[1:], BF16)
            vt_ref[0] = jnp.zeros(vt_ref.shape[1:], BF16)
            k_ref[0, 0:past, :] = ck_ref[0].astype(BF16)
            _store_vt(vt_ref, cv_ref[0].T)
    else:
        project()


def _project(x, mod3, mod_row_fn, norm_w, w_in, q_norm_w, k_norm_w, rope_t, emit_cache, cache,
             q_tile):
    b, n, d = x.shape
    d_in = w_in.shape[1]
    d_att = d_in // 8
    d_vt = d_att // V_DIM * V_EXT
    tm = min(PROJ_TILE, n)
    nt = n // tm
    use_rope = rope_t is not None
    append_cache = cache is not None
    n_keys = n
    tok_t = lambda t: t
    if append_cache:
        assert cache[0].shape[1] <= tm
        n_keys = n + tm
        tok_t = lambda t: jnp.minimum(t, nt - 1)
    qw_b = jnp.broadcast_to(q_norm_w[:, None], (HEAD_DIM, tm))
    kw_b = jnp.broadcast_to(k_norm_w[:, None], (HEAD_DIM, tm))

    in_specs = [
        pl.BlockSpec((1, tm, d), lambda i, t: (i, tok_t(t), 0)),
        pl.BlockSpec((1, 1, d), lambda i, t: (mod_row_fn(i), 0, 0)),
        pl.BlockSpec((1, 1, d), lambda i, t: (mod_row_fn(i), 0, 1)),
        pl.BlockSpec((1, d), lambda i, t: (0, 0)),
        pl.BlockSpec((d, d_in), lambda i, t: (0, 0)),
        pl.BlockSpec((HEAD_DIM, tm), lambda i, t: (0, 0)),
        pl.BlockSpec((HEAD_DIM, tm), lambda i, t: (0, 0)),
    ]
    args = [x, mod3, mod3, norm_w, w_in, qw_b, kw_b]
    if use_rope:
        in_specs += [pl.BlockSpec((ROPE_FREQS, tm), lambda i, t: (0, tok_t(t)))] * 4
        args += list(rope_t)
    if append_cache:
        in_specs += [pl.BlockSpec((1, cache[0].shape[1], d_att), lambda i, t: (i, 0, 0))] * 2
        args += list(cache)

    tok = lambda: pl.BlockSpec((1, tm, d_att), lambda i, t: (i, tok_t(t), 0))
    out_specs = [
        pl.BlockSpec((1, d_att, 2 * tm), lambda i, t: (i, 0, tok_t(t))),
        pl.BlockSpec((1, tm, d_att), lambda i, t: (i, t, 0)),
        pl.BlockSpec((1, d_vt, tm), lambda i, t: (i, 0, t)),
        tok(), tok(), tok(),
    ]
    out_shape = [
        jax.ShapeDtypeStruct((b, d_att, 2 * n), BF16),
        jax.ShapeDtypeStruct((b, n_keys, d_att), BF16),
        jax.ShapeDtypeStruct((b, d_vt, n_keys), BF16),
        jax.ShapeDtypeStruct((b, n, d_att), BF16),
        jax.ShapeDtypeStruct((b, n, d_att), BF16),
        jax.ShapeDtypeStruct((b, n, d_att), BF16),
    ]
    if emit_cache:
        out_specs += [tok(), tok()]
        out_shape += [jax.ShapeDtypeStruct((b, n, d_att), F32)] * 2

    return pl.pallas_call(
        functools.partial(_proj_kernel, use_rope=use_rope, emit_cache=emit_cache,
                          append_cache=append_cache, q_tile=q_tile),
        grid=(b, n_keys // tm),
        in_specs=in_specs,
        out_specs=out_specs,
        out_shape=out_shape,
        compiler_params=pltpu.CompilerParams(
            dimension_semantics=("parallel", "arbitrary"),
            vmem_limit_bytes=VMEM_LIMIT),
        name="project_latent" if use_rope else "project_ctx",
    )(*args)


def _attn_kernel(lam_ref, qt_ref, k_ref, vt_ref, ga_ref, u_ref, up_ref, un_ref, gc_ref, x_ref,
                 gate_ref, sw_ref, cw_ref, cb_ref, wo_ref, out_ref,
                 s_ref, mx_a, p_a, p_b, al_a, st_ref, acc_ref, cat_ref,
                 *, n_new, n_keys, lam_init):
    tq = x_ref.shape[1]
    n_heads = qt_ref.shape[1] // V_DIM
    d_att = n_heads * V_DIM
    lam = lam_ref[0:1, 0:1]

    def conv_branch():
        u = u_ref[0].astype(F32)
        t = pl.program_id(1)
        prev_row = up_ref[0].astype(F32)[BF16_SUBLANES - 1:BF16_SUBLANES, :]
        next_row = un_ref[0].astype(F32)[0:1, :]
        prev_row = jnp.where(t == 0, 0.0, prev_row)
        next_row = jnp.where(t == pl.num_programs(1) - 1, 0.0, next_row)
        rows = lax.broadcasted_iota(jnp.int32, u.shape, 0)
        u_prev = jnp.where(rows == 0, prev_row, pltpu.roll(u, 1, axis=0))
        u_next = jnp.where(rows == tq - 1, next_row, pltpu.roll(u, tq - 1, axis=0))
        conv = (u_prev * cw_ref[0:1, :] + u * cw_ref[1:2, :] + u_next * cw_ref[2:3, :]
                + cb_ref[...])
        conv = (gc_ref[0].astype(F32) * conv).astype(BF16)
        y_conv = jnp.dot(conv, wo_ref[d_att:, :], preferred_element_type=F32)
        out_ref[0] = x_ref[0] + gate_ref[0] * y_conv


    n_full = n_new // KEY_UNIT
    head_units = [(n_new, n_keys - n_new)] if n_keys > n_new else []
    if n_full == 0:
        head_units.append((0, n_new))
    n_units = len(head_units) + n_full

    def unit(i):
        if isinstance(i, int) and i < len(head_units):
            return head_units[i]
        off = (i - len(head_units)) * KEY_UNIT
        return (off if isinstance(i, int) else pl.multiple_of(off, KEY_UNIT)), KEY_UNIT

    def score_dot(unit, h):
        off, rows = unit
        hs = slice(h * V_DIM, (h + 1) * V_DIM)
        return jnp.dot(k_ref[0, pl.ds(off, rows), hs], qt_ref[0, hs, :],
                       preferred_element_type=F32)

    def group_max(s):
        return jnp.max(s.reshape(s.shape[0] // 8, 8, s.shape[1]), axis=0)

    def pv_dot(unit, h, p_ref):
        off, rows = unit
        return jnp.dot(vt_ref[0, h * V_EXT:(h + 1) * V_EXT, pl.ds(off, rows)],
                       p_ref[h, 0:rows, :], preferred_element_type=F32)

    def exact_scores(unit):
        for h in range(n_heads):
            s = score_dot(unit, h)
            s_ref[h, 0:unit[1], :] = s
            mx_a[h] = group_max(s)

    def exact_softmax(unit):
        rows = unit[1]
        for h in range(n_heads):
            m_old = st_ref[h:h + 1, :]
            m_new = jnp.maximum(m_old, jnp.max(mx_a[h], axis=0, keepdims=True))
            st_ref[h:h + 1, :] = m_new
            al_a[h:h + 1, :] = jnp.exp2(m_old - m_new)
            p_a[h, 0:rows, :] = jnp.exp2(s_ref[h, 0:rows, :] - m_new).astype(BF16)

    def exact_accumulate(unit, first):
        for h in range(n_heads):
            pv = pv_dot(unit, h, p_a)
            acc_ref[h] = pv if first else al_a[h:h + 1, :] * acc_ref[h] + pv

    def exact_unit(unit, first=False):
        exact_scores(unit)
        exact_softmax(unit)
        exact_accumulate(unit, first)

    def fast_scores(unit, p_ref):
        for h in range(n_heads):
            p_ref[h, 0:unit[1], :] = jnp.exp2(score_dot(unit, h)).astype(BF16)

    def fast_accumulate(unit, p_ref, first=False):
        for h in range(n_heads):
            pv = pv_dot(unit, h, p_ref)
            acc_ref[h] = pv if first else acc_ref[h] + pv

    def finish():
        bad = jnp.zeros((1, tq), F32)
        for h in range(n_heads):
            hs = slice(h * V_DIM, (h + 1) * V_DIM)
            l0 = acc_ref[h, V_DIM:V_DIM + 1, 0:tq]
            l1 = acc_ref[h, V_DIM:V_DIM + 1, tq:2 * tq]
            o_t = (acc_ref[h, 0:V_DIM, 0:tq] * (1.0 / l0)
                   - acc_ref[h, 0:V_DIM, tq:2 * tq] * (lam / l1))
            for l in (l0, l1):
                bad = (bad + jnp.where(l <= FAST_SUM_LIMIT, 0.0, 1.0)
                       + jnp.where(l >= 1.0 / FAST_SUM_LIMIT, 0.0, 1.0))
            bad = bad + jnp.max(jnp.where(jnp.abs(o_t) < jnp.inf, 0.0, 1.0),
                                axis=0, keepdims=True)
            o = o_t.T
            ms = jnp.mean(o * o, axis=-1, keepdims=True)
            att = o * lax.rsqrt(ms + EPS) * sw_ref[...] * (1.0 - lam_init)
            cat_ref[:, hs] = (att * ga_ref[0, :, hs].astype(F32)).astype(BF16)
        y_att = jnp.dot(cat_ref[...], wo_ref[0:d_att, :], preferred_element_type=F32)
        out_ref[0] += gate_ref[0] * y_att
        return jnp.max(bad)

    def exact_attention():
        st_ref[...] = jnp.full(st_ref.shape, -jnp.inf, F32)
        exact_unit(unit(0), first=True)

        def step(i, carry):
            exact_unit(unit(i))
            return carry

        lax.fori_loop(1, n_units, step, 0)

    if n_units == 1:
        conv_branch()
        exact_attention()
        finish()
    else:
        assert n_units % 2 == 1 and n_units >= 3 and len(head_units) <= 1

        def half(u, new, old, first=False):
            fast_scores(unit(u), new)
            fast_accumulate(unit(u - 1), old, first)

        fast_scores(unit(0), p_a)
        half(1, p_b, p_a, first=True)

        def pair(jj, carry):
            half(2 * jj, p_a, p_b)
            half(2 * jj + 1, p_b, p_a)
            return carry

        lax.fori_loop(1, (n_units - 1) // 2, pair, 0)
        last = n_units - 1
        half(last, p_a, p_b)
        fast_accumulate(unit(last), p_a)
        conv_branch()
        out_of_range = finish()

        @pl.when(out_of_range > 0.0)
        def _():
            exact_attention()
            conv_branch()
            finish()


def _attend(lam, qt, k, vt, n_keys, ga, u, gc, x, mod3, mod_row_fn, subln_w, conv_w, conv_b,
            w_out, lam_init):
    b, n, d = x.shape
    d_att = k.shape[2]
    tq = min(TILE, n)
    nt = n // tq
    halo = BF16_SUBLANES
    blocks_per_tile = tq // halo
    n_halo_blocks = n // halo
    n_heads = d_att // V_DIM
    key_rows = -(-n_keys // LANE) * LANE
    assert key_rows <= k.shape[1]

    tok = lambda: pl.BlockSpec((1, tq, d_att), lambda i, t: (i, t, 0))
    in_specs = [
        pl.BlockSpec((8, LANE), lambda i, t: (0, 0)),
        pl.BlockSpec((1, d_att, 2 * tq), lambda i, t: (i, 0, t)),
        pl.BlockSpec((1, key_rows, d_att), lambda i, t: (i, 0, 0)),
        pl.BlockSpec((1, vt.shape[1], key_rows), lambda i, t: (i, 0, 0)),
        tok(),
        tok(),
        pl.BlockSpec((1, halo, d_att),
                     lambda i, t: (i, jnp.maximum(t * blocks_per_tile - 1, 0), 0)),
        pl.BlockSpec((1, halo, d_att),
                     lambda i, t: (i, jnp.minimum((t + 1) * blocks_per_tile, n_halo_blocks - 1), 0)),
        tok(),
        pl.BlockSpec((1, tq, d), lambda i, t: (i, t, 0)),
        pl.BlockSpec((1, 1, d), lambda i, t: (mod_row_fn(i), 0, 2)),
        pl.BlockSpec((1, V_DIM), lambda i, t: (0, 0)),
        pl.BlockSpec((CONV_W, d_att), lambda i, t: (0, 0)),
        pl.BlockSpec((1, d_att), lambda i, t: (0, 0)),
        pl.BlockSpec((2 * d_att, d), lambda i, t: (0, 0)),
    ]
    args = [lam, qt, k, vt, ga, u, u, u, gc, x, mod3, subln_w, conv_w, conv_b, w_out]
    unit_rows = min(KEY_UNIT, n_keys)
    score_buf = lambda: pltpu.VMEM((n_heads, unit_rows, 2 * tq), F32)
    smax_buf = lambda: pltpu.VMEM((n_heads, 8, 2 * tq), F32)
    prob_buf = lambda: pltpu.VMEM((n_heads, unit_rows, 2 * tq), BF16)
    alpha_buf = lambda: pltpu.VMEM((8, 2 * tq), F32)

    return pl.pallas_call(
        functools.partial(_attn_kernel, n_new=n, n_keys=n_keys, lam_init=lam_init),
        grid=(b, nt),
        in_specs=in_specs,
        out_specs=pl.BlockSpec((1, tq, d), lambda i, t: (i, t, 0)),
        out_shape=jax.ShapeDtypeStruct((b, n, d), F32),
        scratch_shapes=[
            score_buf(), smax_buf(), prob_buf(), prob_buf(),
            alpha_buf(),
            alpha_buf(),
            pltpu.VMEM((n_heads, V_EXT, 2 * tq), F32),
            pltpu.VMEM((tq, d_att), BF16),
        ],
        compiler_params=pltpu.CompilerParams(
            dimension_semantics=("parallel", "arbitrary"),
            vmem_limit_bytes=VMEM_LIMIT),
        name="attend_latent" if n_keys > n else "attend_ctx",
    )(*args)


def _rope_tables_t(n):
    t = jnp.arange(n)
    row = (t // GRID_W).astype(F32)
    col = (t % GRID_W).astype(F32)
    inv = 1.0 / (ROPE_BASE ** (jnp.arange(ROPE_FREQS, dtype=F32) / ROPE_FREQS))
    ang_r = row[:, None] * inv
    ang_c = col[:, None] * inv
    return (jnp.cos(ang_r).T, jnp.sin(ang_r).T, jnp.cos(ang_c).T, jnp.sin(ang_c).T)


def kernel(x_prompt, x_sample, cache_k, cache_v, c, c_ctx, norm_w, w_ada, b_ada, w_in,
           q_norm_w, k_norm_w, lambda_q1, lambda_k1, lambda_q2, lambda_k2, subln_w,
           conv_w, conv_b, w_out):
    depth = norm_w.shape[0]
    assert depth == 1
    i = 0
    lam_init = 0.8 - 0.6 * math.exp(-0.3 * i)
    b_ctx, n_ctx, d = x_prompt.shape
    b_lat, n_lat, _ = x_sample.shape
    d_att = w_in.shape[2] // 8
    n_heads = d_att // V_DIM

    n_rows = 16
    cvecs = jnp.zeros((n_rows, d), F32).at[:b_lat].set(c).at[b_lat].set(c_ctx)
    mod, lam = _modulation(cvecs, w_ada[i], b_ada[i][None], lambda_q1[i][None], lambda_k1[i][None],
                           lambda_q2[i][None], lambda_k2[i][None], lam_init)
    mod3 = mod.reshape(n_rows, 1, 3 * d)

    w_in_b = w_in[i].astype(BF16)
    w_out_b = w_out[i].astype(BF16)
    nw = norm_w[i][None]
    qw, kw = q_norm_w[i], k_norm_w[i]
    sw = subln_w[i][None]
    cw = conv_w[i]
    cb = conv_b[i][None]

    ctx_row = lambda bi: b_lat
    lat_row = lambda bi: bi

    qt, k, vt, ga, u, gc, kf, vf = _project(x_prompt, mod3, ctx_row, nw, w_in_b, qw, kw,
                                            None, True, None, min(TILE, n_ctx))
    y_prompt = _attend(lam, qt, k, vt, n_ctx, ga, u, gc, x_prompt, mod3, ctx_row, sw, cw, cb,
                       w_out_b, lam_init)
    new_cache_k = kf.reshape(b_ctx, 1, n_ctx, n_heads, 2, HEAD_DIM)
    new_cache_v = vf.reshape(b_ctx, 1, n_ctx, n_heads, V_DIM)

    past = cache_k.shape[2]
    cache = (cache_k[:, i].reshape(b_lat, past, d_att), cache_v[:, i].reshape(b_lat, past, d_att))
    qt, k, vt, ga, u, gc = _project(x_sample, mod3, lat_row, nw, w_in_b, qw, kw,
                                    _rope_tables_t(n_lat), False, cache, min(TILE, n_lat))
    y_sample = _attend(lam, qt, k, vt, n_lat + past, ga, u, gc, x_sample, mod3, lat_row, sw, cw, cb,
                       w_out_b, lam_init)
    return (y_prompt, y_sample, new_cache_k, new_cache_v)
```

```python
import functools
import math

import jax
import jax.numpy as jnp
from jax import lax
from jax.experimental import pallas as pl
from jax.experimental.pallas import tpu as pltpu

F32 = jnp.float32
BF16 = jnp.bfloat16

HEAD_DIM = 64
V_DIM = 2 * HEAD_DIM
GRID_W = 64
ROPE_FREQS = HEAD_DIM // 4
ROPE_BASE = 10000.0
EPS = 1e-6
CONV_W = 3
LANE = 128
BF16_SUBLANES = 16
VMEM_LIMIT = 56 * 1024 * 1024
TILE = 512
PROJ_TILE = 1024
KEY_UNIT = 256
FAST_KEY_UNIT = 512
FAST_SUM_LIMIT = 2.0 ** 60
Q_SCALE = HEAD_DIM ** -0.5 * math.log2(math.e)
V_EXT = V_DIM + BF16_SUBLANES


def _silu(x):
    return x * (1.0 / (1.0 + jnp.exp(-x)))


def _mod_kernel(c_ref, w_ref, b_ref, lq1, lk1, lq2, lk2, mod_ref, lam_ref, *, lam_init):
    a = _silu(c_ref[...]).astype(BF16)
    m = jnp.dot(a, w_ref[...].astype(BF16), preferred_element_type=F32)
    mod_ref[...] = m + b_ref[...]
    e1 = jnp.exp(jnp.sum(lq1[...] * lk1[...], axis=-1, keepdims=True))
    e2 = jnp.exp(jnp.sum(lq2[...] * lk2[...], axis=-1, keepdims=True))
    lam_ref[...] = jnp.broadcast_to(e1 - e2 + lam_init, lam_ref.shape)


def _modulation(cvecs, w_ada, b_ada, lq1, lk1, lq2, lk2, lam_init):
    rows, d = cvecs.shape
    n_out = w_ada.shape[1]
    bn = 512
    vec = lambda: pl.BlockSpec((1, HEAD_DIM), lambda j: (0, 0))
    return pl.pallas_call(
        functools.partial(_mod_kernel, lam_init=lam_init),
        grid=(n_out // bn,),
        in_specs=[
            pl.BlockSpec((rows, d), lambda j: (0, 0)),
            pl.BlockSpec((d, bn), lambda j: (0, j)),
            pl.BlockSpec((1, bn), lambda j: (0, j)),
            vec(), vec(), vec(), vec(),
        ],
        out_specs=[
            pl.BlockSpec((rows, bn), lambda j: (0, j)),
            pl.BlockSpec((8, LANE), lambda j: (0, 0)),
        ],
        out_shape=[
            jax.ShapeDtypeStruct((rows, n_out), F32),
            jax.ShapeDtypeStruct((8, LANE), F32),
        ],
        name="modulation",
    )(cvecs, w_ada, b_ada, lq1, lk1, lq2, lk2)


def _norm_rope_t(z_t, w_b, rope):
    outs = []
    f = ROPE_FREQS
    for g in range(z_t.shape[0] // HEAD_DIM):
        zg = z_t[g * HEAD_DIM:(g + 1) * HEAD_DIM, :]
        ms = jnp.mean(zg * zg, axis=0, keepdims=True)
        yg = zg * lax.rsqrt(ms + EPS) * w_b
        if rope is not None:
            cr, sr, cc, sc = rope
            x1r, x2r, x1c, x2c = yg[0:f], yg[f:2 * f], yg[2 * f:3 * f], yg[3 * f:4 * f]
            yg = jnp.concatenate([x1r * cr - x2r * sr, x2r * cr + x1r * sr,
                                  x1c * cc - x2c * sc, x2c * cc + x1c * sc], axis=0)
        outs.append(yg)
    return outs


def _store_vt(vt_ref, v_t):
    t = v_t.shape[1]
    ones_blk = (lax.broadcasted_iota(jnp.int32, (V_EXT - V_DIM, t), 0) == 0).astype(BF16)
    for h in range(v_t.shape[0] // V_DIM):
        vt_ref[0, h * V_EXT:h * V_EXT + V_DIM, 0:t] = v_t[h * V_DIM:(h + 1) * V_DIM].astype(BF16)
        vt_ref[0, h * V_EXT + V_DIM:(h + 1) * V_EXT, 0:t] = ones_blk


def _proj_kernel(*refs, use_rope, emit_cache, append_cache, q_tile):
    it = iter(refs)
    x_ref, shift_ref, scale_ref, nw_ref, w_ref, qw_ref, kw_ref = (next(it) for _ in range(7))
    rope_refs = tuple(next(it) for _ in range(4)) if use_rope else None
    if append_cache:
        ck_ref, cv_ref = next(it), next(it)
    qt_ref, k_ref, vt_ref, ga_ref, u_ref, gc_ref = (next(it) for _ in range(6))
    if emit_cache:
        kf_ref, vf_ref = next(it), next(it)

    def project():
        x = x_ref[0]
        ms = jnp.mean(x * x, axis=-1, keepdims=True)
        y = x * lax.rsqrt(ms + EPS) * nw_ref[...]
        h = (y * (1.0 + scale_ref[0]) + shift_ref[0]).astype(BF16)
        rope = tuple(r[...] for r in rope_refs) if use_rope else None
        d_att = k_ref.shape[2]

        def zcol(i):
            return jnp.dot(h, w_ref[:, i * d_att:(i + 1) * d_att], preferred_element_type=F32)

        q_groups = _norm_rope_t(zcol(0).T, qw_ref[...], rope)
        zero = jnp.zeros_like(q_groups[0])
        for g, yg in enumerate(q_groups):
            head, m = divmod(g, 2)
            yg = yg * Q_SCALE
            padded = jnp.concatenate([yg, zero] if m == 0 else [zero, yg], axis=0).astype(BF16)
            for j in range(padded.shape[1] // q_tile):
                col = (2 * j + m) * q_tile
                qt_ref[0, head * V_DIM:(head + 1) * V_DIM, col:col + q_tile] = (
                    padded[:, j * q_tile:(j + 1) * q_tile])

        k = jnp.concatenate(_norm_rope_t(zcol(1).T, kw_ref[...], rope), axis=0).T
        k_ref[0] = k.astype(BF16)
        v = zcol(2)
        _store_vt(vt_ref, v.T)
        if emit_cache:
            kf_ref[0] = k
            vf_ref[0] = v

        ga_ref[0] = _silu(zcol(3)).astype(BF16)
        cb = zcol(4)
        u_ref[0] = (zcol(5) * zcol(6)).astype(BF16)
        gc_ref[0] = (cb * _silu(zcol(7))).astype(BF16)

    if append_cache:
        last = pl.num_programs(1) - 1
        pl.when(pl.program_id(1) < last)(project)

        @pl.when(pl.program_id(1) == last)
        def _():
            past = ck_ref.shape[1]
            k_ref[0] = jnp.zeros(k_ref.shape[1:], BF16)
            vt_ref[0] = jnp.zeros(vt_ref.shape[1:], BF16)
            k_ref[0, 0:past, :] = ck_ref[0].astype(BF16)
            _store_vt(vt_ref, cv_ref[0].T)
    else:
        project()


def _project(x, mod3, mod_row_fn, norm_w, w_in, q_norm_w, k_norm_w, rope_t, emit_cache, cache,
             q_tile):
    b, n, d = x.shape
    d_in = w_in.shape[1]
    d_att = d_in // 8
    d_vt = d_att // V_DIM * V_EXT
    tm = min(PROJ_TILE, n)
    nt = n // tm
    use_rope = rope_t is not None
    append_cache = cache is not None
    n_keys = n
    tok_t = lambda t: t
    if append_cache:
        assert cache[0].shape[1] <= tm
        n_keys = n + tm
        tok_t = lambda t: jnp.minimum(t, nt - 1)
    qw_b = jnp.broadcast_to(q_norm_w[:, None], (HEAD_DIM, tm))
    kw_b = jnp.broadcast_to(k_norm_w[:, None], (HEAD_DIM, tm))

    in_specs = [
        pl.BlockSpec((1, tm, d), lambda i, t: (i, tok_t(t), 0)),
        pl.BlockSpec((1, 1, d), lambda i, t: (mod_row_fn(i), 0, 0)),
        pl.BlockSpec((1, 1, d), lambda i, t: (mod_row_fn(i), 0, 1)),
        pl.BlockSpec((1, d), lambda i, t: (0, 0)),
        pl.BlockSpec((d, d_in), lambda i, t: (0, 0)),
        pl.BlockSpec((HEAD_DIM, tm), lambda i, t: (0, 0)),
        pl.BlockSpec((HEAD_DIM, tm), lambda i, t: (0, 0)),
    ]
    args = [x, mod3, mod3, norm_w, w_in, qw_b, kw_b]
    if use_rope:
        in_specs += [pl.BlockSpec((ROPE_FREQS, tm), lambda i, t: (0, tok_t(t)))] * 4
        args += list(rope_t)
    if append_cache:
        in_specs += [pl.BlockSpec((1, cache[0].shape[1], d_att), lambda i, t: (i, 0, 0))] * 2
        args += list(cache)

    tok = lambda: pl.BlockSpec((1, tm, d_att), lambda i, t: (i, tok_t(t), 0))
    out_specs = [
        pl.BlockSpec((1, d_att, 2 * tm), lambda i, t: (i, 0, tok_t(t))),
        pl.BlockSpec((1, tm, d_att), lambda i, t: (i, t, 0)),
        pl.BlockSpec((1, d_vt, tm), lambda i, t: (i, 0, t)),
        tok(), tok(), tok(),
    ]
    out_shape = [
        jax.ShapeDtypeStruct((b, d_att, 2 * n), BF16),
        jax.ShapeDtypeStruct((b, n_keys, d_att), BF16),
        jax.ShapeDtypeStruct((b, d_vt, n_keys), BF16),
        jax.ShapeDtypeStruct((b, n, d_att), BF16),
        jax.ShapeDtypeStruct((b, n, d_att), BF16),
        jax.ShapeDtypeStruct((b, n, d_att), BF16),
    ]
    if emit_cache:
        out_specs += [tok(), tok()]
        out_shape += [jax.ShapeDtypeStruct((b, n, d_att), F32)] * 2

    return pl.pallas_call(
        functools.partial(_proj_kernel, use_rope=use_rope, emit_cache=emit_cache,
                          append_cache=append_cache, q_tile=q_tile),
        grid=(b, n_keys // tm),
        in_specs=in_specs,
        out_specs=out_specs,
        out_shape=out_shape,
        compiler_params=pltpu.CompilerParams(
            dimension_semantics=("parallel", "arbitrary"),
            vmem_limit_bytes=VMEM_LIMIT),
        name="project_latent" if use_rope else "project_ctx",
    )(*args)


def _attn_kernel(lam_ref, qt_ref, k_ref, vt_ref, ga_ref, u_ref, up_ref, un_ref, gc_ref, x_ref,
                 gate_ref, sw_ref, cw_ref, cb_ref, wo_ref, out_ref,
                 s_ref, mx_a, p_a, p_b, al_a, st_ref, acc_ref, cat_ref,
                 *, n_new, n_keys, lam_init):
    tq = x_ref.shape[1]
    n_heads = qt_ref.shape[1] // V_DIM
    d_att = n_heads * V_DIM
    lam = lam_ref[0:1, 0:1]

    def conv_branch():
        u = u_ref[0].astype(F32)
        t = pl.program_id(1)
        prev_row = up_ref[0].astype(F32)[BF16_SUBLANES - 1:BF16_SUBLANES, :]
        next_row = un_ref[0].astype(F32)[0:1, :]
        prev_row = jnp.where(t == 0, 0.0, prev_row)
        next_row = jnp.where(t == pl.num_programs(1) - 1, 0.0, next_row)
        rows = lax.broadcasted_iota(jnp.int32, u.shape, 0)
        u_prev = jnp.where(rows == 0, prev_row, pltpu.roll(u, 1, axis=0))
        u_next = jnp.where(rows == tq - 1, next_row, pltpu.roll(u, tq - 1, axis=0))
        conv = (u_prev * cw_ref[0:1, :] + u * cw_ref[1:2, :] + u_next * cw_ref[2:3, :]
                + cb_ref[...])
        conv = (gc_ref[0].astype(F32) * conv).astype(BF16)
        y_conv = jnp.dot(conv, wo_ref[d_att:, :], preferred_element_type=F32)
        out_ref[0] = x_ref[0] + gate_ref[0] * y_conv


    def key_units(size):
        n_full = n_new // size
        head_units = [(n_new, n_keys - n_new)] if n_keys > n_new else []
        if n_full == 0:
            head_units.append((0, n_new))
        assert len(head_units) <= 1

        def unit(i):
            if isinstance(i, int) and i < len(head_units):
                return head_units[i]
            off = (i - len(head_units)) * size
            return (off if isinstance(i, int) else pl.multiple_of(off, size)), size

        return len(head_units) + n_full, unit

    n_exact_units, exact_key_unit = key_units(KEY_UNIT)
    n_units, unit = key_units(FAST_KEY_UNIT)

    def score_dot(unit, h):
        off, rows = unit
        hs = slice(h * V_DIM, (h + 1) * V_DIM)
        return jnp.dot(k_ref[0, pl.ds(off, rows), hs], qt_ref[0, hs, :],
                       preferred_element_type=F32)

    def group_max(s):
        return jnp.max(s.reshape(s.shape[0] // 8, 8, s.shape[1]), axis=0)

    def pv_dot(unit, h, p_ref):
        off, rows = unit
        return jnp.dot(vt_ref[0, h * V_EXT:(h + 1) * V_EXT, pl.ds(off, rows)],
                       p_ref[h, 0:rows, :], preferred_element_type=F32)

    def exact_scores(unit):
        for h in range(n_heads):
            s = score_dot(unit, h)
            s_ref[h, 0:unit[1], :] = s
            mx_a[h] = group_max(s)

    def exact_softmax(unit):
        rows = unit[1]
        for h in range(n_heads):
            m_old = st_ref[h:h + 1, :]
            m_new = jnp.maximum(m_old, jnp.max(mx_a[h], axis=0, keepdims=True))
            st_ref[h:h + 1, :] = m_new
            al_a[h:h + 1, :] = jnp.exp2(m_old - m_new)
            p_a[h, 0:rows, :] = jnp.exp2(s_ref[h, 0:rows, :] - m_new).astype(BF16)

    def exact_accumulate(unit, first):
        for h in range(n_heads):
            pv = pv_dot(unit, h, p_a)
            acc_ref[h] = pv if first else al_a[h:h + 1, :] * acc_ref[h] + pv

    def exact_unit(unit, first=False):
        exact_scores(unit)
        exact_softmax(unit)
        exact_accumulate(unit, first)

    def fast_scores(unit, p_ref):
        for h in range(n_heads):
            p_ref[h, 0:unit[1], :] = jnp.exp2(score_dot(unit, h)).astype(BF16)

    def fast_accumulate(unit, p_ref, first=False):
        for h in range(n_heads):
            pv = pv_dot(unit, h, p_ref)
            acc_ref[h] = pv if first else acc_ref[h] + pv

    def finish():
        bad = jnp.zeros((1, tq), F32)
        for h in range(n_heads):
            hs = slice(h * V_DIM, (h + 1) * V_DIM)
            l0 = acc_ref[h, V_DIM:V_DIM + 1, 0:tq]
            l1 = acc_ref[h, V_DIM:V_DIM + 1, tq:2 * tq]
            o_t = (acc_ref[h, 0:V_DIM, 0:tq] * (1.0 / l0)
                   - acc_ref[h, 0:V_DIM, tq:2 * tq] * (lam / l1))
            for l in (l0, l1):
                bad = (bad + jnp.where(l <= FAST_SUM_LIMIT, 0.0, 1.0)
                       + jnp.where(l >= 1.0 / FAST_SUM_LIMIT, 0.0, 1.0))
            bad = bad + jnp.max(jnp.where(jnp.abs(o_t) < jnp.inf, 0.0, 1.0),
                                axis=0, keepdims=True)
            o = o_t.T
            ms = jnp.mean(o * o, axis=-1, keepdims=True)
            att = o * lax.rsqrt(ms + EPS) * sw_ref[...] * (1.0 - lam_init)
            cat_ref[:, hs] = (att * ga_ref[0, :, hs].astype(F32)).astype(BF16)
        y_att = jnp.dot(cat_ref[...], wo_ref[0:d_att, :], preferred_element_type=F32)
        out_ref[0] += gate_ref[0] * y_att
        return jnp.max(bad)

    def exact_attention():
        st_ref[...] = jnp.full(st_ref.shape, -jnp.inf, F32)
        exact_unit(exact_key_unit(0), first=True)

        def step(i, carry):
            exact_unit(exact_key_unit(i))
            return carry

        lax.fori_loop(1, n_exact_units, step, 0)

    if n_exact_units == 1:
        conv_branch()
        exact_attention()
        finish()
    else:
        assert n_units % 2 == 1 and n_units >= 3

        def half(u, new, old, first=False):
            fast_scores(unit(u), new)
            fast_accumulate(unit(u - 1), old, first)

        fast_scores(unit(0), p_a)
        half(1, p_b, p_a, first=True)

        def pair(jj, carry):
            half(2 * jj, p_a, p_b)
            half(2 * jj + 1, p_b, p_a)
            return carry

        lax.fori_loop(1, (n_units - 1) // 2, pair, 0)
        last = n_units - 1
        half(last, p_a, p_b)
        fast_accumulate(unit(last), p_a)
        conv_branch()
        out_of_range = finish()

        @pl.when(out_of_range > 0.0)
        def _():
            exact_attention()
            conv_branch()
            finish()


def _attend(lam, qt, k, vt, n_keys, ga, u, gc, x, mod3, mod_row_fn, subln_w, conv_w, conv_b,
            w_out, lam_init):
    b, n, d = x.shape
    d_att = k.shape[2]
    tq = min(TILE, n)
    nt = n // tq
    halo = BF16_SUBLANES
    blocks_per_tile = tq // halo
    n_halo_blocks = n // halo
    n_heads = d_att // V_DIM
    key_rows = -(-n_keys // LANE) * LANE
    assert key_rows <= k.shape[1]

    tok = lambda: pl.BlockSpec((1, tq, d_att), lambda i, t: (i, t, 0))
    in_specs = [
        pl.BlockSpec((8, LANE), lambda i, t: (0, 0)),
        pl.BlockSpec((1, d_att, 2 * tq), lambda i, t: (i, 0, t)),
        pl.BlockSpec((1, key_rows, d_att), lambda i, t: (i, 0, 0)),
        pl.BlockSpec((1, vt.shape[1], key_rows), lambda i, t: (i, 0, 0)),
        tok(),
        tok(),
        pl.BlockSpec((1, halo, d_att),
                     lambda i, t: (i, jnp.maximum(t * blocks_per_tile - 1, 0), 0)),
        pl.BlockSpec((1, halo, d_att),
                     lambda i, t: (i, jnp.minimum((t + 1) * blocks_per_tile, n_halo_blocks - 1), 0)),
        tok(),
        pl.BlockSpec((1, tq, d), lambda i, t: (i, t, 0)),
        pl.BlockSpec((1, 1, d), lambda i, t: (mod_row_fn(i), 0, 2)),
        pl.BlockSpec((1, V_DIM), lambda i, t: (0, 0)),
        pl.BlockSpec((CONV_W, d_att), lambda i, t: (0, 0)),
        pl.BlockSpec((1, d_att), lambda i, t: (0, 0)),
        pl.BlockSpec((2 * d_att, d), lambda i, t: (0, 0)),
    ]
    args = [lam, qt, k, vt, ga, u, u, u, gc, x, mod3, subln_w, conv_w, conv_b, w_out]
    unit_rows = min(KEY_UNIT, n_keys)
    score_buf = lambda: pltpu.VMEM((n_heads, unit_rows, 2 * tq), F32)
    smax_buf = lambda: pltpu.VMEM((n_heads, 8, 2 * tq), F32)
    prob_buf = lambda: pltpu.VMEM((n_heads, min(FAST_KEY_UNIT, n_keys), 2 * tq), BF16)
    alpha_buf = lambda: pltpu.VMEM((8, 2 * tq), F32)

    return pl.pallas_call(
        functools.partial(_attn_kernel, n_new=n, n_keys=n_keys, lam_init=lam_init),
        grid=(b, nt),
        in_specs=in_specs,
        out_specs=pl.BlockSpec((1, tq, d), lambda i, t: (i, t, 0)),
        out_shape=jax.ShapeDtypeStruct((b, n, d), F32),
        scratch_shapes=[
            score_buf(), smax_buf(), prob_buf(), prob_buf(),
            alpha_buf(),
            alpha_buf(),
            pltpu.VMEM((n_heads, V_EXT, 2 * tq), F32),
            pltpu.VMEM((tq, d_att), BF16),
        ],
        compiler_params=pltpu.CompilerParams(
            dimension_semantics=("parallel", "arbitrary"),
            vmem_limit_bytes=VMEM_LIMIT),
        name="attend_latent" if n_keys > n else "attend_ctx",
    )(*args)


def _rope_tables_t(n):
    t = jnp.arange(n)
    row = (t // GRID_W).astype(F32)
    col = (t % GRID_W).astype(F32)
    inv = 1.0 / (ROPE_BASE ** (jnp.arange(ROPE_FREQS, dtype=F32) / ROPE_FREQS))
    ang_r = row[:, None] * inv
    ang_c = col[:, None] * inv
    return (jnp.cos(ang_r).T, jnp.sin(ang_r).T, jnp.cos(ang_c).T, jnp.sin(ang_c).T)


def kernel(x_prompt, x_sample, cache_k, cache_v, c, c_ctx, norm_w, w_ada, b_ada, w_in,
           q_norm_w, k_norm_w, lambda_q1, lambda_k1, lambda_q2, lambda_k2, subln_w,
           conv_w, conv_b, w_out):
    depth = norm_w.shape[0]
    assert depth == 1
    i = 0
    lam_init = 0.8 - 0.6 * math.exp(-0.3 * i)
    b_ctx, n_ctx, d = x_prompt.shape
    b_lat, n_lat, _ = x_sample.shape
    d_att = w_in.shape[2] // 8
    n_heads = d_att // V_DIM

    n_rows = 16
    cvecs = jnp.zeros((n_rows, d), F32).at[:b_lat].set(c).at[b_lat].set(c_ctx)
    mod, lam = _modulation(cvecs, w_ada[i], b_ada[i][None], lambda_q1[i][None], lambda_k1[i][None],
                           lambda_q2[i][None], lambda_k2[i][None], lam_init)
    mod3 = mod.reshape(n_rows, 1, 3 * d)

    w_in_b = w_in[i].astype(BF16)
    w_out_b = w_out[i].astype(BF16)
    nw = norm_w[i][None]
    qw, kw = q_norm_w[i], k_norm_w[i]
    sw = subln_w[i][None]
    cw = conv_w[i]
    cb = conv_b[i][None]

    ctx_row = lambda bi: b_lat
    lat_row = lambda bi: bi

    qt, k, vt, ga, u, gc, kf, vf = _project(x_prompt, mod3, ctx_row, nw, w_in_b, qw, kw,
                                            None, True, None, min(TILE, n_ctx))
    y_prompt = _attend(lam, qt, k, vt, n_ctx, ga, u, gc, x_prompt, mod3, ctx_row, sw, cw, cb,
                       w_out_b, lam_init)
    new_cache_k = kf.reshape(b_ctx, 1, n_ctx, n_heads, 2, HEAD_DIM)
    new_cache_v = vf.reshape(b_ctx, 1, n_ctx, n_heads, V_DIM)

    past = cache_k.shape[2]
    cache = (cache_k[:, i].reshape(b_lat, past, d_att), cache_v[:, i].reshape(b_lat, past, d_att))
    qt, k, vt, ga, u, gc = _project(x_sample, mod3, lat_row, nw, w_in_b, qw, kw,
                                    _rope_tables_t(n_lat), False, cache, min(TILE, n_lat))
    y_sample = _attend(lam, qt, k, vt, n_lat + past, ga, u, gc, x_sample, mod3, lat_row, sw, cw, cb,
                       w_out_b, lam_init)
    return (y_prompt, y_sample, new_cache_k, new_cache_v)
```

```python
import functools
import math

import jax
import jax.numpy as jnp
from jax import lax
from jax.experimental import pallas as pl
from jax.experimental.pallas import tpu as pltpu

F32 = jnp.float32
BF16 = jnp.bfloat16

HEAD_DIM = 64
V_DIM = 2 * HEAD_DIM
GRID_W = 64
ROPE_FREQS = HEAD_DIM // 4
ROPE_BASE = 10000.0
EPS = 1e-6
CONV_W = 3
LANE = 128
BF16_SUBLANES = 16
VMEM_LIMIT = 56 * 1024 * 1024
TILE = 512
PROJ_TILE = 1024
KEY_UNIT = 256
FAST_KEY_UNIT = 512
FAST_SUM_LIMIT = 2.0 ** 60
Q_SCALE = HEAD_DIM ** -0.5 * math.log2(math.e)


def _silu(x):
    return x * (1.0 / (1.0 + jnp.exp(-x)))


def _mod_kernel(c_ref, w_ref, b_ref, lq1, lk1, lq2, lk2, mod_ref, lam_ref, *, lam_init):
    a = _silu(c_ref[...]).astype(BF16)
    m = jnp.dot(a, w_ref[...].astype(BF16), preferred_element_type=F32)
    mod_ref[...] = m + b_ref[...]
    e1 = jnp.exp(jnp.sum(lq1[...] * lk1[...], axis=-1, keepdims=True))
    e2 = jnp.exp(jnp.sum(lq2[...] * lk2[...], axis=-1, keepdims=True))
    lam_ref[...] = jnp.broadcast_to(e1 - e2 + lam_init, lam_ref.shape)


def _modulation(cvecs, w_ada, b_ada, lq1, lk1, lq2, lk2, lam_init):
    rows, d = cvecs.shape
    n_out = w_ada.shape[1]
    bn = 512
    vec = lambda: pl.BlockSpec((1, HEAD_DIM), lambda j: (0, 0))
    return pl.pallas_call(
        functools.partial(_mod_kernel, lam_init=lam_init),
        grid=(n_out // bn,),
        in_specs=[
            pl.BlockSpec((rows, d), lambda j: (0, 0)),
            pl.BlockSpec((d, bn), lambda j: (0, j)),
            pl.BlockSpec((1, bn), lambda j: (0, j)),
            vec(), vec(), vec(), vec(),
        ],
        out_specs=[
            pl.BlockSpec((rows, bn), lambda j: (0, j)),
            pl.BlockSpec((8, LANE), lambda j: (0, 0)),
        ],
        out_shape=[
            jax.ShapeDtypeStruct((rows, n_out), F32),
            jax.ShapeDtypeStruct((8, LANE), F32),
        ],
        name="modulation",
    )(cvecs, w_ada, b_ada, lq1, lk1, lq2, lk2)


def _norm_rope_t(z_t, w_b, rope):
    outs = []
    f = ROPE_FREQS
    for g in range(z_t.shape[0] // HEAD_DIM):
        zg = z_t[g * HEAD_DIM:(g + 1) * HEAD_DIM, :]
        ms = jnp.mean(zg * zg, axis=0, keepdims=True)
        yg = zg * lax.rsqrt(ms + EPS) * w_b
        if rope is not None:
            cr, sr, cc, sc = rope
            x1r, x2r, x1c, x2c = yg[0:f], yg[f:2 * f], yg[2 * f:3 * f], yg[3 * f:4 * f]
            yg = jnp.concatenate([x1r * cr - x2r * sr, x2r * cr + x1r * sr,
                                  x1c * cc - x2c * sc, x2c * cc + x1c * sc], axis=0)
        outs.append(yg)
    return outs


def _store_vt(vt_ref, v_t):
    vt_ref[0, :, 0:v_t.shape[1]] = v_t.astype(BF16)


def _proj_kernel(*refs, use_rope, emit_cache, append_cache, q_tile):
    it = iter(refs)
    x_ref, shift_ref, scale_ref, nw_ref, w_ref, qw_ref, kw_ref = (next(it) for _ in range(7))
    rope_refs = tuple(next(it) for _ in range(4)) if use_rope else None
    if append_cache:
        ck_ref, cv_ref = next(it), next(it)
    qt_ref, k_ref, vt_ref, ga_ref, u_ref, gc_ref = (next(it) for _ in range(6))
    if emit_cache:
        kf_ref, vf_ref = next(it), next(it)

    def project():
        x = x_ref[0]
        ms = jnp.mean(x * x, axis=-1, keepdims=True)
        y = x * lax.rsqrt(ms + EPS) * nw_ref[...]
        h = (y * (1.0 + scale_ref[0]) + shift_ref[0]).astype(BF16)
        rope = tuple(r[...] for r in rope_refs) if use_rope else None
        d_att = k_ref.shape[2]

        def zcol(i):
            return jnp.dot(h, w_ref[:, i * d_att:(i + 1) * d_att], preferred_element_type=F32)

        q_groups = _norm_rope_t(zcol(0).T, qw_ref[...], rope)
        zero = jnp.zeros_like(q_groups[0])
        for g, yg in enumerate(q_groups):
            head, m = divmod(g, 2)
            yg = yg * Q_SCALE
            padded = jnp.concatenate([yg, zero] if m == 0 else [zero, yg], axis=0).astype(BF16)
            for j in range(padded.shape[1] // q_tile):
                col = (2 * j + m) * q_tile
                qt_ref[0, head * V_DIM:(head + 1) * V_DIM, col:col + q_tile] = (
                    padded[:, j * q_tile:(j + 1) * q_tile])

        k = jnp.concatenate(_norm_rope_t(zcol(1).T, kw_ref[...], rope), axis=0).T
        k_ref[0] = k.astype(BF16)
        v = zcol(2)
        _store_vt(vt_ref, v.T)
        if emit_cache:
            kf_ref[0] = k
            vf_ref[0] = v

        ga_ref[0] = _silu(zcol(3)).astype(BF16)
        cb = zcol(4)
        u_ref[0] = (zcol(5) * zcol(6)).astype(BF16)
        gc_ref[0] = (cb * _silu(zcol(7))).astype(BF16)

    if append_cache:
        last = pl.num_programs(1) - 1
        pl.when(pl.program_id(1) < last)(project)

        @pl.when(pl.program_id(1) == last)
        def _():
            past = ck_ref.shape[1]
            k_ref[0] = jnp.zeros(k_ref.shape[1:], BF16)
            vt_ref[0] = jnp.zeros(vt_ref.shape[1:], BF16)
            k_ref[0, 0:past, :] = ck_ref[0].astype(BF16)
            _store_vt(vt_ref, cv_ref[0].T)
    else:
        project()


def _project(x, mod3, mod_row_fn, norm_w, w_in, q_norm_w, k_norm_w, rope_t, emit_cache, cache,
             q_tile):
    b, n, d = x.shape
    d_in = w_in.shape[1]
    d_att = d_in // 8
    d_vt = d_att
    tm = min(PROJ_TILE, n)
    nt = n // tm
    use_rope = rope_t is not None
    append_cache = cache is not None
    n_keys = n
    tok_t = lambda t: t
    if append_cache:
        assert cache[0].shape[1] <= tm
        n_keys = n + tm
        tok_t = lambda t: jnp.minimum(t, nt - 1)
    qw_b = jnp.broadcast_to(q_norm_w[:, None], (HEAD_DIM, tm))
    kw_b = jnp.broadcast_to(k_norm_w[:, None], (HEAD_DIM, tm))

    in_specs = [
        pl.BlockSpec((1, tm, d), lambda i, t: (i, tok_t(t), 0)),
        pl.BlockSpec((1, 1, d), lambda i, t: (mod_row_fn(i), 0, 0)),
        pl.BlockSpec((1, 1, d), lambda i, t: (mod_row_fn(i), 0, 1)),
        pl.BlockSpec((1, d), lambda i, t: (0, 0)),
        pl.BlockSpec((d, d_in), lambda i, t: (0, 0)),
        pl.BlockSpec((HEAD_DIM, tm), lambda i, t: (0, 0)),
        pl.BlockSpec((HEAD_DIM, tm), lambda i, t: (0, 0)),
    ]
    args = [x, mod3, mod3, norm_w, w_in, qw_b, kw_b]
    if use_rope:
        in_specs += [pl.BlockSpec((ROPE_FREQS, tm), lambda i, t: (0, tok_t(t)))] * 4
        args += list(rope_t)
    if append_cache:
        in_specs += [pl.BlockSpec((1, cache[0].shape[1], d_att), lambda i, t: (i, 0, 0))] * 2
        args += list(cache)

    tok = lambda: pl.BlockSpec((1, tm, d_att), lambda i, t: (i, tok_t(t), 0))
    out_specs = [
        pl.BlockSpec((1, d_att, 2 * tm), lambda i, t: (i, 0, tok_t(t))),
        pl.BlockSpec((1, tm, d_att), lambda i, t: (i, t, 0)),
        pl.BlockSpec((1, d_vt, tm), lambda i, t: (i, 0, t)),
        tok(), tok(), tok(),
    ]
    out_shape = [
        jax.ShapeDtypeStruct((b, d_att, 2 * n), BF16),
        jax.ShapeDtypeStruct((b, n_keys, d_att), BF16),
        jax.ShapeDtypeStruct((b, d_vt, n_keys), BF16),
        jax.ShapeDtypeStruct((b, n, d_att), BF16),
        jax.ShapeDtypeStruct((b, n, d_att), BF16),
        jax.ShapeDtypeStruct((b, n, d_att), BF16),
    ]
    if emit_cache:
        out_specs += [tok(), tok()]
        out_shape += [jax.ShapeDtypeStruct((b, n, d_att), F32)] * 2

    return pl.pallas_call(
        functools.partial(_proj_kernel, use_rope=use_rope, emit_cache=emit_cache,
                          append_cache=append_cache, q_tile=q_tile),
        grid=(b, n_keys // tm),
        in_specs=in_specs,
        out_specs=out_specs,
        out_shape=out_shape,
        compiler_params=pltpu.CompilerParams(
            dimension_semantics=("parallel", "arbitrary"),
            vmem_limit_bytes=VMEM_LIMIT),
        name="project_latent" if use_rope else "project_ctx",
    )(*args)


def _attn_kernel(lam_ref, qt_ref, k_ref, vt_ref, ga_ref, u_ref, up_ref, un_ref, gc_ref, x_ref,
                 gate_ref, sw_ref, cw_ref, cb_ref, wo_ref, out_ref,
                 s_ref, mx_a, p_a, p_b, al_a, st_ref, l_ref, acc_ref, cat_ref,
                 *, n_new, n_keys, lam_init):
    tq = x_ref.shape[1]
    n_heads = qt_ref.shape[1] // V_DIM
    d_att = n_heads * V_DIM
    lam = lam_ref[0:1, 0:1]

    def conv_branch():
        u = u_ref[0].astype(F32)
        t = pl.program_id(1)
        prev_row = up_ref[0].astype(F32)[BF16_SUBLANES - 1:BF16_SUBLANES, :]
        next_row = un_ref[0].astype(F32)[0:1, :]
        prev_row = jnp.where(t == 0, 0.0, prev_row)
        next_row = jnp.where(t == pl.num_programs(1) - 1, 0.0, next_row)
        rows = lax.broadcasted_iota(jnp.int32, u.shape, 0)
        u_prev = jnp.where(rows == 0, prev_row, pltpu.roll(u, 1, axis=0))
        u_next = jnp.where(rows == tq - 1, next_row, pltpu.roll(u, tq - 1, axis=0))
        conv = (u_prev * cw_ref[0:1, :] + u * cw_ref[1:2, :] + u_next * cw_ref[2:3, :]
                + cb_ref[...])
        conv = (gc_ref[0].astype(F32) * conv).astype(BF16)
        y_conv = jnp.dot(conv, wo_ref[d_att:, :], preferred_element_type=F32)
        out_ref[0] = x_ref[0] + gate_ref[0] * y_conv


    def key_units(size):
        n_full = n_new // size
        head_units = [(n_new, n_keys - n_new)] if n_keys > n_new else []
        if n_full == 0:
            head_units.append((0, n_new))
        assert len(head_units) <= 1

        def unit(i):
            if isinstance(i, int) and i < len(head_units):
                return head_units[i]
            off = (i - len(head_units)) * size
            return (off if isinstance(i, int) else pl.multiple_of(off, size)), size

        return len(head_units) + n_full, unit

    n_exact_units, exact_key_unit = key_units(KEY_UNIT)
    n_units, unit = key_units(FAST_KEY_UNIT)

    def score_dot(unit, h):
        off, rows = unit
        hs = slice(h * V_DIM, (h + 1) * V_DIM)
        return jnp.dot(k_ref[0, pl.ds(off, rows), hs], qt_ref[0, hs, :],
                       preferred_element_type=F32)

    def group_max(s):
        return jnp.max(s.reshape(s.shape[0] // 8, 8, s.shape[1]), axis=0)

    def pv_dot(unit, h, p_ref):
        off, rows = unit
        hs = slice(h * V_DIM, (h + 1) * V_DIM)
        return jnp.dot(vt_ref[0, hs, pl.ds(off, rows)], p_ref[h, 0:rows, :],
                       preferred_element_type=F32)

    def exact_scores(unit):
        for h in range(n_heads):
            s = score_dot(unit, h)
            s_ref[h, 0:unit[1], :] = s
            mx_a[h] = group_max(s)

    def exact_softmax(unit, first):
        rows = unit[1]
        for h in range(n_heads):
            m_old = st_ref[h:h + 1, :]
            m_new = jnp.maximum(m_old, jnp.max(mx_a[h], axis=0, keepdims=True))
            st_ref[h:h + 1, :] = m_new
            alpha = jnp.exp2(m_old - m_new)
            al_a[h:h + 1, :] = alpha
            p = jnp.exp2(s_ref[h, 0:rows, :] - m_new)
            p_sum = jnp.sum(p, axis=0, keepdims=True)
            l_ref[h:h + 1, :] = p_sum if first else alpha * l_ref[h:h + 1, :] + p_sum
            p_a[h, 0:rows, :] = p.astype(BF16)

    def exact_accumulate(unit, first):
        for h in range(n_heads):
            pv = pv_dot(unit, h, p_a)
            acc_ref[h] = pv if first else al_a[h:h + 1, :] * acc_ref[h] + pv

    def exact_unit(unit, first=False):
        exact_scores(unit)
        exact_softmax(unit, first)
        exact_accumulate(unit, first)

    def fast_scores(unit, p_ref, first=False):
        for h in range(n_heads):
            p = jnp.exp2(score_dot(unit, h))
            p_sum = jnp.sum(p, axis=0, keepdims=True)
            l_ref[h:h + 1, :] = p_sum if first else l_ref[h:h + 1, :] + p_sum
            p_ref[h, 0:unit[1], :] = p.astype(BF16)

    def fast_accumulate(unit, p_ref, first=False):
        for h in range(n_heads):
            pv = pv_dot(unit, h, p_ref)
            acc_ref[h] = pv if first else acc_ref[h] + pv

    def finish():
        bad = jnp.zeros((1, tq), F32)
        for h in range(n_heads):
            hs = slice(h * V_DIM, (h + 1) * V_DIM)
            l0 = l_ref[h:h + 1, 0:tq]
            l1 = l_ref[h:h + 1, tq:2 * tq]
            o_t = (acc_ref[h, :, 0:tq] * (1.0 / l0)
                   - acc_ref[h, :, tq:2 * tq] * (lam / l1))
            for l in (l0, l1):
                bad = (bad + jnp.where(l <= FAST_SUM_LIMIT, 0.0, 1.0)
                       + jnp.where(l >= 1.0 / FAST_SUM_LIMIT, 0.0, 1.0))
            bad = bad + jnp.max(jnp.where(jnp.abs(o_t) < jnp.inf, 0.0, 1.0),
                                axis=0, keepdims=True)
            o = o_t.T
            ms = jnp.mean(o * o, axis=-1, keepdims=True)
            att = o * lax.rsqrt(ms + EPS) * sw_ref[...] * (1.0 - lam_init)
            cat_ref[:, hs] = (att * ga_ref[0, :, hs].astype(F32)).astype(BF16)
        y_att = jnp.dot(cat_ref[...], wo_ref[0:d_att, :], preferred_element_type=F32)
        out_ref[0] += gate_ref[0] * y_att
        return jnp.max(bad)

    def exact_attention():
        st_ref[...] = jnp.full(st_ref.shape, -jnp.inf, F32)
        exact_unit(exact_key_unit(0), first=True)

        def step(i, carry):
            exact_unit(exact_key_unit(i))
            return carry

        lax.fori_loop(1, n_exact_units, step, 0)

    if n_exact_units == 1:
        conv_branch()
        exact_attention()
        finish()
    else:
        assert n_units % 2 == 1 and n_units >= 3

        def half(u, new, old, first=False):
            fast_scores(unit(u), new)
            fast_accumulate(unit(u - 1), old, first)

        fast_scores(unit(0), p_a, first=True)
        half(1, p_b, p_a, first=True)

        def pair(jj, carry):
            half(2 * jj, p_a, p_b)
            half(2 * jj + 1, p_b, p_a)
            return carry

        lax.fori_loop(1, (n_units - 1) // 2, pair, 0)
        last = n_units - 1
        half(last, p_a, p_b)
        fast_accumulate(unit(last), p_a)
        conv_branch()
        out_of_range = finish()

        @pl.when(out_of_range > 0.0)
        def _():
            exact_attention()
            conv_branch()
            finish()


def _attend(lam, qt, k, vt, n_keys, ga, u, gc, x, mod3, mod_row_fn, subln_w, conv_w, conv_b,
            w_out, lam_init):
    b, n, d = x.shape
    d_att = k.shape[2]
    tq = min(TILE, n)
    nt = n // tq
    halo = BF16_SUBLANES
    blocks_per_tile = tq // halo
    n_halo_blocks = n // halo
    n_heads = d_att // V_DIM
    key_rows = -(-n_keys // LANE) * LANE
    assert key_rows <= k.shape[1]

    tok = lambda: pl.BlockSpec((1, tq, d_att), lambda i, t: (i, t, 0))
    in_specs = [
        pl.BlockSpec((8, LANE), lambda i, t: (0, 0)),
        pl.BlockSpec((1, d_att, 2 * tq), lambda i, t: (i, 0, t)),
        pl.BlockSpec((1, key_rows, d_att), lambda i, t: (i, 0, 0)),
        pl.BlockSpec((1, vt.shape[1], key_rows), lambda i, t: (i, 0, 0)),
        tok(),
        tok(),
        pl.BlockSpec((1, halo, d_att),
                     lambda i, t: (i, jnp.maximum(t * blocks_per_tile - 1, 0), 0)),
        pl.BlockSpec((1, halo, d_att),
                     lambda i, t: (i, jnp.minimum((t + 1) * blocks_per_tile, n_halo_blocks - 1), 0)),
        tok(),
        pl.BlockSpec((1, tq, d), lambda i, t: (i, t, 0)),
        pl.BlockSpec((1, 1, d), lambda i, t: (mod_row_fn(i), 0, 2)),
        pl.BlockSpec((1, V_DIM), lambda i, t: (0, 0)),
        pl.BlockSpec((CONV_W, d_att), lambda i, t: (0, 0)),
        pl.BlockSpec((1, d_att), lambda i, t: (0, 0)),
        pl.BlockSpec((2 * d_att, d), lambda i, t: (0, 0)),
    ]
    args = [lam, qt, k, vt, ga, u, u, u, gc, x, mod3, subln_w, conv_w, conv_b, w_out]
    unit_rows = min(KEY_UNIT, n_keys)
    score_buf = lambda: pltpu.VMEM((n_heads, unit_rows, 2 * tq), F32)
    smax_buf = lambda: pltpu.VMEM((n_heads, 8, 2 * tq), F32)
    prob_buf = lambda: pltpu.VMEM((n_heads, min(FAST_KEY_UNIT, n_keys), 2 * tq), BF16)
    alpha_buf = lambda: pltpu.VMEM((8, 2 * tq), F32)

    return pl.pallas_call(
        functools.partial(_attn_kernel, n_new=n, n_keys=n_keys, lam_init=lam_init),
        grid=(b, nt),
        in_specs=in_specs,
        out_specs=pl.BlockSpec((1, tq, d), lambda i, t: (i, t, 0)),
        out_shape=jax.ShapeDtypeStruct((b, n, d), F32),
        scratch_shapes=[
            score_buf(), smax_buf(), prob_buf(), prob_buf(),
            alpha_buf(),
            alpha_buf(),
            alpha_buf(),
            pltpu.VMEM((n_heads, V_DIM, 2 * tq), F32),
            pltpu.VMEM((tq, d_att), BF16),
        ],
        compiler_params=pltpu.CompilerParams(
            dimension_semantics=("parallel", "arbitrary"),
            vmem_limit_bytes=VMEM_LIMIT),
        name="attend_latent" if n_keys > n else "attend_ctx",
    )(*args)


def _rope_tables_t(n):
    t = jnp.arange(n)
    row = (t // GRID_W).astype(F32)
    col = (t % GRID_W).astype(F32)
    inv = 1.0 / (ROPE_BASE ** (jnp.arange(ROPE_FREQS, dtype=F32) / ROPE_FREQS))
    ang_r = row[:, None] * inv
    ang_c = col[:, None] * inv
    return (jnp.cos(ang_r).T, jnp.sin(ang_r).T, jnp.cos(ang_c).T, jnp.sin(ang_c).T)


def kernel(x_prompt, x_sample, cache_k, cache_v, c, c_ctx, norm_w, w_ada, b_ada, w_in,
           q_norm_w, k_norm_w, lambda_q1, lambda_k1, lambda_q2, lambda_k2, subln_w,
           conv_w, conv_b, w_out):
    depth = norm_w.shape[0]
    assert depth == 1
    i = 0
    lam_init = 0.8 - 0.6 * math.exp(-0.3 * i)
    b_ctx, n_ctx, d = x_prompt.shape
    b_lat, n_lat, _ = x_sample.shape
    d_att = w_in.shape[2] // 8
    n_heads = d_att // V_DIM

    n_rows = 16
    cvecs = jnp.zeros((n_rows, d), F32).at[:b_lat].set(c).at[b_lat].set(c_ctx)
    mod, lam = _modulation(cvecs, w_ada[i], b_ada[i][None], lambda_q1[i][None], lambda_k1[i][None],
                           lambda_q2[i][None], lambda_k2[i][None], lam_init)
    mod3 = mod.reshape(n_rows, 1, 3 * d)

    w_in_b = w_in[i].astype(BF16)
    w_out_b = w_out[i].astype(BF16)
    nw = norm_w[i][None]
    qw, kw = q_norm_w[i], k_norm_w[i]
    sw = subln_w[i][None]
    cw = conv_w[i]
    cb = conv_b[i][None]

    ctx_row = lambda bi: b_lat
    lat_row = lambda bi: bi

    qt, k, vt, ga, u, gc, kf, vf = _project(x_prompt, mod3, ctx_row, nw, w_in_b, qw, kw,
                                            None, True, None, min(TILE, n_ctx))
    y_prompt = _attend(lam, qt, k, vt, n_ctx, ga, u, gc, x_prompt, mod3, ctx_row, sw, cw, cb,
                       w_out_b, lam_init)
    new_cache_k = kf.reshape(b_ctx, 1, n_ctx, n_heads, 2, HEAD_DIM)
    new_cache_v = vf.reshape(b_ctx, 1, n_ctx, n_heads, V_DIM)

    past = cache_k.shape[2]
    cache = (cache_k[:, i].reshape(b_lat, past, d_att), cache_v[:, i].reshape(b_lat, past, d_att))
    qt, k, vt, ga, u, gc = _project(x_sample, mod3, lat_row, nw, w_in_b, qw, kw,
                                    _rope_tables_t(n_lat), False, cache, min(TILE, n_lat))
    y_sample = _attend(lam, qt, k, vt, n_lat + past, ga, u, gc, x_sample, mod3, lat_row, sw, cw, cb,
                       w_out_b, lam_init)
    return (y_prompt, y_sample, new_cache_k, new_cache_v)
```

```python
import functools
import math

import jax
import jax.numpy as jnp
from jax import lax
from jax.experimental import pallas as pl
from jax.experimental.pallas import tpu as pltpu

F32 = jnp.float32
BF16 = jnp.bfloat16

HEAD_DIM = 64
V_DIM = 2 * HEAD_DIM
GRID_W = 64
ROPE_FREQS = HEAD_DIM // 4
ROPE_BASE = 10000.0
EPS = 1e-6
CONV_W = 3
LANE = 128
BF16_SUBLANES = 16
VMEM_LIMIT = 56 * 1024 * 1024
TILE = 512
PROJ_TILE = 1024
KEY_UNIT = 256
FAST_KEY_UNIT = 512
FAST_SUM_LIMIT = 2.0 ** 60
Q_SCALE = HEAD_DIM ** -0.5 * math.log2(math.e)


def _silu(x):
    return x * (1.0 / (1.0 + jnp.exp(-x)))


def _mod_kernel(c_ref, w_ref, b_ref, lq1, lk1, lq2, lk2, mod_ref, lam_ref, *, lam_init):
    a = _silu(c_ref[...]).astype(BF16)
    m = jnp.dot(a, w_ref[...].astype(BF16), preferred_element_type=F32)
    mod_ref[...] = m + b_ref[...]
    e1 = jnp.exp(jnp.sum(lq1[...] * lk1[...], axis=-1, keepdims=True))
    e2 = jnp.exp(jnp.sum(lq2[...] * lk2[...], axis=-1, keepdims=True))
    lam_ref[...] = jnp.broadcast_to(e1 - e2 + lam_init, lam_ref.shape)


def _modulation(cvecs, w_ada, b_ada, lq1, lk1, lq2, lk2, lam_init):
    rows, d = cvecs.shape
    n_out = w_ada.shape[1]
    bn = 512
    vec = lambda: pl.BlockSpec((1, HEAD_DIM), lambda j: (0, 0))
    return pl.pallas_call(
        functools.partial(_mod_kernel, lam_init=lam_init),
        grid=(n_out // bn,),
        in_specs=[
            pl.BlockSpec((rows, d), lambda j: (0, 0)),
            pl.BlockSpec((d, bn), lambda j: (0, j)),
            pl.BlockSpec((1, bn), lambda j: (0, j)),
            vec(), vec(), vec(), vec(),
        ],
        out_specs=[
            pl.BlockSpec((rows, bn), lambda j: (0, j)),
            pl.BlockSpec((8, LANE), lambda j: (0, 0)),
        ],
        out_shape=[
            jax.ShapeDtypeStruct((rows, n_out), F32),
            jax.ShapeDtypeStruct((8, LANE), F32),
        ],
        name="modulation",
    )(cvecs, w_ada, b_ada, lq1, lk1, lq2, lk2)


def _norm_rope_t(z_t, w_b, rope):
    outs = []
    f = ROPE_FREQS
    for g in range(z_t.shape[0] // HEAD_DIM):
        zg = z_t[g * HEAD_DIM:(g + 1) * HEAD_DIM, :]
        ms = jnp.mean(zg * zg, axis=0, keepdims=True)
        yg = zg * lax.rsqrt(ms + EPS) * w_b
        if rope is not None:
            cr, sr, cc, sc = rope
            x1r, x2r, x1c, x2c = yg[0:f], yg[f:2 * f], yg[2 * f:3 * f], yg[3 * f:4 * f]
            yg = jnp.concatenate([x1r * cr - x2r * sr, x2r * cr + x1r * sr,
                                  x1c * cc - x2c * sc, x2c * cc + x1c * sc], axis=0)
        outs.append(yg)
    return outs


def _store_vt(vt_ref, v_t):
    vt_ref[0, :, 0:v_t.shape[1]] = v_t.astype(BF16)


def _proj_kernel(*refs, use_rope, emit_cache, append_cache, q_tile):
    it = iter(refs)
    x_ref, shift_ref, scale_ref, nw_ref, w_ref, qw_ref, kw_ref = (next(it) for _ in range(7))
    rope_refs = tuple(next(it) for _ in range(4)) if use_rope else None
    if append_cache:
        ck_ref, cv_ref = next(it), next(it)
    qt_ref, k_ref, vt_ref, ga_ref, u_ref, gc_ref = (next(it) for _ in range(6))
    if emit_cache:
        kf_ref, vf_ref = next(it), next(it)

    def project():
        x = x_ref[0]
        ms = jnp.mean(x * x, axis=-1, keepdims=True)
        y = x * lax.rsqrt(ms + EPS) * nw_ref[...]
        h = (y * (1.0 + scale_ref[0]) + shift_ref[0]).astype(BF16)
        rope = tuple(r[...] for r in rope_refs) if use_rope else None
        d_att = k_ref.shape[2]

        def zcol(i):
            return jnp.dot(h, w_ref[:, i * d_att:(i + 1) * d_att], preferred_element_type=F32)

        q_groups = _norm_rope_t(zcol(0).T, qw_ref[...], rope)
        zero = jnp.zeros_like(q_groups[0])
        for g, yg in enumerate(q_groups):
            head, m = divmod(g, 2)
            yg = yg * Q_SCALE
            padded = jnp.concatenate([yg, zero] if m == 0 else [zero, yg], axis=0).astype(BF16)
            for j in range(padded.shape[1] // q_tile):
                col = (2 * j + m) * q_tile
                qt_ref[0, head * V_DIM:(head + 1) * V_DIM, col:col + q_tile] = (
                    padded[:, j * q_tile:(j + 1) * q_tile])

        k = jnp.concatenate(_norm_rope_t(zcol(1).T, kw_ref[...], rope), axis=0).T
        k_ref[0] = k.astype(BF16)
        v = zcol(2)
        _store_vt(vt_ref, v.T)
        if emit_cache:
            kf_ref[0] = k
            vf_ref[0] = v

        ga_ref[0] = _silu(zcol(3)).astype(BF16)
        cb = zcol(4)
        u_ref[0] = (zcol(5) * zcol(6)).astype(BF16)
        gc_ref[0] = (cb * _silu(zcol(7))).astype(BF16)

    if append_cache:
        last = pl.num_programs(1) - 1
        pl.when(pl.program_id(1) < last)(project)

        @pl.when(pl.program_id(1) == last)
        def _():
            past = ck_ref.shape[1]
            k_ref[0] = jnp.zeros(k_ref.shape[1:], BF16)
            vt_ref[0] = jnp.zeros(vt_ref.shape[1:], BF16)
            k_ref[0, 0:past, :] = ck_ref[0].astype(BF16)
            _store_vt(vt_ref, cv_ref[0].T)
    else:
        project()


def _project(x, mod3, mod_row_fn, norm_w, w_in, q_norm_w, k_norm_w, rope_t, emit_cache, cache,
             q_tile):
    b, n, d = x.shape
    d_in = w_in.shape[1]
    d_att = d_in // 8
    d_vt = d_att
    tm = min(PROJ_TILE, n)
    nt = n // tm
    use_rope = rope_t is not None
    append_cache = cache is not None
    n_keys = n
    tok_t = lambda t: t
    if append_cache:
        assert cache[0].shape[1] <= tm
        n_keys = n + tm
        tok_t = lambda t: jnp.minimum(t, nt - 1)
    qw_b = jnp.broadcast_to(q_norm_w[:, None], (HEAD_DIM, tm))
    kw_b = jnp.broadcast_to(k_norm_w[:, None], (HEAD_DIM, tm))

    in_specs = [
        pl.BlockSpec((1, tm, d), lambda i, t: (i, tok_t(t), 0)),
        pl.BlockSpec((1, 1, d), lambda i, t: (mod_row_fn(i), 0, 0)),
        pl.BlockSpec((1, 1, d), lambda i, t: (mod_row_fn(i), 0, 1)),
        pl.BlockSpec((1, d), lambda i, t: (0, 0)),
        pl.BlockSpec((d, d_in), lambda i, t: (0, 0)),
        pl.BlockSpec((HEAD_DIM, tm), lambda i, t: (0, 0)),
        pl.BlockSpec((HEAD_DIM, tm), lambda i, t: (0, 0)),
    ]
    args = [x, mod3, mod3, norm_w, w_in, qw_b, kw_b]
    if use_rope:
        in_specs += [pl.BlockSpec((ROPE_FREQS, tm), lambda i, t: (0, tok_t(t)))] * 4
        args += list(rope_t)
    if append_cache:
        in_specs += [pl.BlockSpec((1, cache[0].shape[1], d_att), lambda i, t: (i, 0, 0))] * 2
        args += list(cache)

    tok = lambda: pl.BlockSpec((1, tm, d_att), lambda i, t: (i, tok_t(t), 0))
    out_specs = [
        pl.BlockSpec((1, d_att, 2 * tm), lambda i, t: (i, 0, tok_t(t))),
        pl.BlockSpec((1, tm, d_att), lambda i, t: (i, t, 0)),
        pl.BlockSpec((1, d_vt, tm), lambda i, t: (i, 0, t)),
        tok(), tok(), tok(),
    ]
    out_shape = [
        jax.ShapeDtypeStruct((b, d_att, 2 * n), BF16),
        jax.ShapeDtypeStruct((b, n_keys, d_att), BF16),
        jax.ShapeDtypeStruct((b, d_vt, n_keys), BF16),
        jax.ShapeDtypeStruct((b, n, d_att), BF16),
        jax.ShapeDtypeStruct((b, n, d_att), BF16),
        jax.ShapeDtypeStruct((b, n, d_att), BF16),
    ]
    if emit_cache:
        out_specs += [tok(), tok()]
        out_shape += [jax.ShapeDtypeStruct((b, n, d_att), F32)] * 2

    return pl.pallas_call(
        functools.partial(_proj_kernel, use_rope=use_rope, emit_cache=emit_cache,
                          append_cache=append_cache, q_tile=q_tile),
        grid=(b, n_keys // tm),
        in_specs=in_specs,
        out_specs=out_specs,
        out_shape=out_shape,
        compiler_params=pltpu.CompilerParams(
            dimension_semantics=("parallel", "arbitrary"),
            vmem_limit_bytes=VMEM_LIMIT),
        name="project_latent" if use_rope else "project_ctx",
    )(*args)


def _attn_kernel(lam_ref, qt_ref, k_ref, vt_ref, ga_ref, u_ref, up_ref, un_ref, gc_ref, x_ref,
                 gate_ref, sw_ref, cw_ref, cb_ref, wo_ref, out_ref,
                 s_ref, mx_a, p_a, p_b, al_a, st_ref, l_ref, acc_ref, cat_ref,
                 *, n_new, n_keys, lam_init):
    tq = x_ref.shape[1]
    n_heads = qt_ref.shape[1] // V_DIM
    d_att = n_heads * V_DIM
    lam = lam_ref[0:1, 0:1]

    def conv_branch():
        u = u_ref[0].astype(F32)
        t = pl.program_id(1)
        prev_row = up_ref[0].astype(F32)[BF16_SUBLANES - 1:BF16_SUBLANES, :]
        next_row = un_ref[0].astype(F32)[0:1, :]
        prev_row = jnp.where(t == 0, 0.0, prev_row)
        next_row = jnp.where(t == pl.num_programs(1) - 1, 0.0, next_row)
        rows = lax.broadcasted_iota(jnp.int32, u.shape, 0)
        u_prev = jnp.where(rows == 0, prev_row, pltpu.roll(u, 1, axis=0))
        u_next = jnp.where(rows == tq - 1, next_row, pltpu.roll(u, tq - 1, axis=0))
        conv = (u_prev * cw_ref[0:1, :] + u * cw_ref[1:2, :] + u_next * cw_ref[2:3, :]
                + cb_ref[...])
        conv = (gc_ref[0].astype(F32) * conv).astype(BF16)
        y_conv = jnp.dot(conv, wo_ref[d_att:, :], preferred_element_type=F32)
        out_ref[0] = x_ref[0] + gate_ref[0] * y_conv


    def key_units(size):
        n_full = n_new // size
        head_units = [(n_new, n_keys - n_new)] if n_keys > n_new else []
        if n_full == 0:
            head_units.append((0, n_new))
        assert len(head_units) <= 1

        def unit(i):
            if isinstance(i, int) and i < len(head_units):
                return head_units[i]
            off = (i - len(head_units)) * size
            return (off if isinstance(i, int) else pl.multiple_of(off, size)), size

        return len(head_units) + n_full, unit

    n_exact_units, exact_key_unit = key_units(KEY_UNIT)
    n_units, unit = key_units(FAST_KEY_UNIT)

    def score_dot(unit, h):
        off, rows = unit
        hs = slice(h * V_DIM, (h + 1) * V_DIM)
        return jnp.dot(k_ref[0, pl.ds(off, rows), hs], qt_ref[0, hs, :],
                       preferred_element_type=F32)

    def group_max(s):
        return jnp.max(s.reshape(s.shape[0] // 8, 8, s.shape[1]), axis=0)

    def pv_dot(unit, h, p_ref):
        off, rows = unit
        hs = slice(h * V_DIM, (h + 1) * V_DIM)
        return jnp.dot(vt_ref[0, hs, pl.ds(off, rows)], p_ref[h, 0:rows, :],
                       preferred_element_type=F32)

    def exact_scores(unit):
        for h in range(n_heads):
            s = score_dot(unit, h)
            s_ref[h, 0:unit[1], :] = s
            mx_a[h] = group_max(s)

    def exact_softmax(unit, first):
        rows = unit[1]
        for h in range(n_heads):
            m_old = st_ref[h:h + 1, :]
            m_new = jnp.maximum(m_old, jnp.max(mx_a[h], axis=0, keepdims=True))
            st_ref[h:h + 1, :] = m_new
            alpha = jnp.exp2(m_old - m_new)
            al_a[h:h + 1, :] = alpha
            p = jnp.exp2(s_ref[h, 0:rows, :] - m_new)
            p_sum = jnp.sum(p, axis=0, keepdims=True)
            l_ref[h:h + 1, :] = p_sum if first else alpha * l_ref[h:h + 1, :] + p_sum
            p_a[h, 0:rows, :] = p.astype(BF16)

    def exact_accumulate(unit, first):
        for h in range(n_heads):
            pv = pv_dot(unit, h, p_a)
            acc_ref[h] = pv if first else al_a[h:h + 1, :] * acc_ref[h] + pv

    def exact_unit(unit, first=False):
        exact_scores(unit)
        exact_softmax(unit, first)
        exact_accumulate(unit, first)

    def fast_scores(unit, p_ref, first=False):
        for h in range(n_heads):
            p = jnp.exp2(score_dot(unit, h))
            p_sum = jnp.sum(p, axis=0, keepdims=True)
            l_ref[h:h + 1, :] = p_sum if first else l_ref[h:h + 1, :] + p_sum
            p_ref[h, 0:unit[1], :] = p.astype(BF16)

    def fast_accumulate(unit, p_ref, first=False):
        for h in range(n_heads):
            pv = pv_dot(unit, h, p_ref)
            acc_ref[h] = pv if first else acc_ref[h] + pv

    def finish():
        bad = jnp.zeros((1, tq), F32)
        for h in range(n_heads):
            hs = slice(h * V_DIM, (h + 1) * V_DIM)
            l0 = l_ref[h:h + 1, 0:tq]
            l1 = l_ref[h:h + 1, tq:2 * tq]
            o_t = (acc_ref[h, :, 0:tq] * (1.0 / l0)
                   - acc_ref[h, :, tq:2 * tq] * (lam / l1))
            for l in (l0, l1):
                bad = (bad + jnp.where(l <= FAST_SUM_LIMIT, 0.0, 1.0)
                       + jnp.where(l >= 1.0 / FAST_SUM_LIMIT, 0.0, 1.0))
            bad = bad + jnp.max(jnp.where(jnp.abs(o_t) < jnp.inf, 0.0, 1.0),
                                axis=0, keepdims=True)
            o = o_t.T
            ms = jnp.mean(o * o, axis=-1, keepdims=True)
            att = o * lax.rsqrt(ms + EPS) * sw_ref[...] * (1.0 - lam_init)
            cat_ref[:, hs] = (att * ga_ref[0, :, hs].astype(F32)).astype(BF16)
        y_att = jnp.dot(cat_ref[...], wo_ref[0:d_att, :], preferred_element_type=F32)
        out_ref[0] += gate_ref[0] * y_att
        return jnp.max(bad)

    def exact_attention():
        st_ref[...] = jnp.full(st_ref.shape, -jnp.inf, F32)
        exact_unit(exact_key_unit(0), first=True)

        def step(i, carry):
            exact_unit(exact_key_unit(i))
            return carry

        lax.fori_loop(1, n_exact_units, step, 0)

    if n_units == 1:
        fast_scores(unit(0), p_a, first=True)
        fast_accumulate(unit(0), p_a, first=True)
    else:
        assert n_units % 2 == 1 and n_units >= 3

        def half(u, new, old, first=False):
            fast_scores(unit(u), new)
            fast_accumulate(unit(u - 1), old, first)

        fast_scores(unit(0), p_a, first=True)
        half(1, p_b, p_a, first=True)

        def pair(jj, carry):
            half(2 * jj, p_a, p_b)
            half(2 * jj + 1, p_b, p_a)
            return carry

        lax.fori_loop(1, (n_units - 1) // 2, pair, 0)
        last = n_units - 1
        half(last, p_a, p_b)
        fast_accumulate(unit(last), p_a)
    conv_branch()
    out_of_range = finish()

    @pl.when(out_of_range > 0.0)
    def _():
        exact_attention()
        conv_branch()
        finish()


def _attend(lam, qt, k, vt, n_keys, ga, u, gc, x, mod3, mod_row_fn, subln_w, conv_w, conv_b,
            w_out, lam_init):
    b, n, d = x.shape
    d_att = k.shape[2]
    tq = min(TILE, n)
    nt = n // tq
    halo = BF16_SUBLANES
    blocks_per_tile = tq // halo
    n_halo_blocks = n // halo
    n_heads = d_att // V_DIM
    key_rows = -(-n_keys // LANE) * LANE
    assert key_rows <= k.shape[1]

    tok = lambda: pl.BlockSpec((1, tq, d_att), lambda i, t: (i, t, 0))
    in_specs = [
        pl.BlockSpec((8, LANE), lambda i, t: (0, 0)),
        pl.BlockSpec((1, d_att, 2 * tq), lambda i, t: (i, 0, t)),
        pl.BlockSpec((1, key_rows, d_att), lambda i, t: (i, 0, 0)),
        pl.BlockSpec((1, vt.shape[1], key_rows), lambda i, t: (i, 0, 0)),
        tok(),
        tok(),
        pl.BlockSpec((1, halo, d_att),
                     lambda i, t: (i, jnp.maximum(t * blocks_per_tile - 1, 0), 0)),
        pl.BlockSpec((1, halo, d_att),
                     lambda i, t: (i, jnp.minimum((t + 1) * blocks_per_tile, n_halo_blocks - 1), 0)),
        tok(),
        pl.BlockSpec((1, tq, d), lambda i, t: (i, t, 0)),
        pl.BlockSpec((1, 1, d), lambda i, t: (mod_row_fn(i), 0, 2)),
        pl.BlockSpec((1, V_DIM), lambda i, t: (0, 0)),
        pl.BlockSpec((CONV_W, d_att), lambda i, t: (0, 0)),
        pl.BlockSpec((1, d_att), lambda i, t: (0, 0)),
        pl.BlockSpec((2 * d_att, d), lambda i, t: (0, 0)),
    ]
    args = [lam, qt, k, vt, ga, u, u, u, gc, x, mod3, subln_w, conv_w, conv_b, w_out]
    unit_rows = min(KEY_UNIT, n_keys)
    score_buf = lambda: pltpu.VMEM((n_heads, unit_rows, 2 * tq), F32)
    smax_buf = lambda: pltpu.VMEM((n_heads, 8, 2 * tq), F32)
    prob_buf = lambda: pltpu.VMEM((n_heads, min(FAST_KEY_UNIT, n_keys), 2 * tq), BF16)
    alpha_buf = lambda: pltpu.VMEM((8, 2 * tq), F32)

    return pl.pallas_call(
        functools.partial(_attn_kernel, n_new=n, n_keys=n_keys, lam_init=lam_init),
        grid=(b, nt),
        in_specs=in_specs,
        out_specs=pl.BlockSpec((1, tq, d), lambda i, t: (i, t, 0)),
        out_shape=jax.ShapeDtypeStruct((b, n, d), F32),
        scratch_shapes=[
            score_buf(), smax_buf(), prob_buf(), prob_buf(),
            alpha_buf(),
            alpha_buf(),
            alpha_buf(),
            pltpu.VMEM((n_heads, V_DIM, 2 * tq), F32),
            pltpu.VMEM((tq, d_att), BF16),
        ],
        compiler_params=pltpu.CompilerParams(
            dimension_semantics=("parallel", "arbitrary"),
            vmem_limit_bytes=VMEM_LIMIT),
        name="attend_latent" if n_keys > n else "attend_ctx",
    )(*args)


def _rope_tables_t(n):
    t = jnp.arange(n)
    row = (t // GRID_W).astype(F32)
    col = (t % GRID_W).astype(F32)
    inv = 1.0 / (ROPE_BASE ** (jnp.arange(ROPE_FREQS, dtype=F32) / ROPE_FREQS))
    ang_r = row[:, None] * inv
    ang_c = col[:, None] * inv
    return (jnp.cos(ang_r).T, jnp.sin(ang_r).T, jnp.cos(ang_c).T, jnp.sin(ang_c).T)


def kernel(x_prompt, x_sample, cache_k, cache_v, c, c_ctx, norm_w, w_ada, b_ada, w_in,
           q_norm_w, k_norm_w, lambda_q1, lambda_k1, lambda_q2, lambda_k2, subln_w,
           conv_w, conv_b, w_out):
    depth = norm_w.shape[0]
    assert depth == 1
    i = 0
    lam_init = 0.8 - 0.6 * math.exp(-0.3 * i)
    b_ctx, n_ctx, d = x_prompt.shape
    b_lat, n_lat, _ = x_sample.shape
    d_att = w_in.shape[2] // 8
    n_heads = d_att // V_DIM

    n_rows = 16
    cvecs = jnp.zeros((n_rows, d), F32).at[:b_lat].set(c).at[b_lat].set(c_ctx)
    mod, lam = _modulation(cvecs, w_ada[i], b_ada[i][None], lambda_q1[i][None], lambda_k1[i][None],
                           lambda_q2[i][None], lambda_k2[i][None], lam_init)
    mod3 = mod.reshape(n_rows, 1, 3 * d)

    w_in_b = w_in[i].astype(BF16)
    w_out_b = w_out[i].astype(BF16)
    nw = norm_w[i][None]
    qw, kw = q_norm_w[i], k_norm_w[i]
    sw = subln_w[i][None]
    cw = conv_w[i]
    cb = conv_b[i][None]

    ctx_row = lambda bi: b_lat
    lat_row = lambda bi: bi

    qt, k, vt, ga, u, gc, kf, vf = _project(x_prompt, mod3, ctx_row, nw, w_in_b, qw, kw,
                                            None, True, None, min(TILE, n_ctx))
    y_prompt = _attend(lam, qt, k, vt, n_ctx, ga, u, gc, x_prompt, mod3, ctx_row, sw, cw, cb,
                       w_out_b, lam_init)
    new_cache_k = kf.reshape(b_ctx, 1, n_ctx, n_heads, 2, HEAD_DIM)
    new_cache_v = vf.reshape(b_ctx, 1, n_ctx, n_heads, V_DIM)

    past = cache_k.shape[2]
    cache = (cache_k[:, i].reshape(b_lat, past, d_att), cache_v[:, i].reshape(b_lat, past, d_att))
    qt, k, vt, ga, u, gc = _project(x_sample, mod3, lat_row, nw, w_in_b, qw, kw,
                                    _rope_tables_t(n_lat), False, cache, min(TILE, n_lat))
    y_sample = _attend(lam, qt, k, vt, n_lat + past, ga, u, gc, x_sample, mod3, lat_row, sw, cw, cb,
                       w_out_b, lam_init)
    return (y_prompt, y_sample, new_cache_k, new_cache_v)
```
